```python
import jax, jax.numpy as jnp
from jax import lax
import numpy as np

D_MODEL = 1024
BATCH = 8
SEQ = 2048
DEPTH = 1
DEC_BATCH = 128
DEC_SEQ = 4
PAST_LEN = 16384
PAGE_SIZE = 128

D_CONV = D_MODEL
CONV_W = 31
D_POOL = D_MODEL
POOL_WINDOWS = (2, 4, 8, 16)
N_POOL_GROUPS = len(POOL_WINDOWS)
POOL_GROUP = D_POOL // N_POOL_GROUPS
MAX_POOL = max(POOL_WINDOWS)
D_FF = 4 * D_MODEL
N_MOD = 6
D_IN = 2 * D_CONV + D_POOL + 2 * D_MODEL
EPS = 1e-6

kernel_name = "conv_pool_gated_hybrid_step"


def _rms(x):
    xf = x.astype(jnp.float32)
    return xf * lax.rsqrt(jnp.mean(xf * xf, axis=-1, keepdims=True) + EPS)


def _layer(x, c, conv_hist, pool_hist, pos0, w_ada, b_ada, w_in, conv_w, conv_b, ln_g, ln_b,
           w_conv_out, pool_mix, pool_scale, w_pool_out, w_out, w_ff1, w_ff2):
    dt = x.dtype
    n, t, _ = x.shape
    mod = (jax.nn.silu(c) @ w_ada + b_ada)[:, None, :]
    sh1, sc1, g1, sh2, sc2, g2 = jnp.split(mod, N_MOD, axis=-1)

    h = (_rms(x) * (1.0 + sc1) + sh1).astype(dt)
    z = h @ w_in
    a_val, a_gate, u, ga, gb = jnp.split(
        z, np.cumsum([D_CONV, D_CONV, D_POOL, D_MODEL]).tolist(), axis=-1)

    a = a_val * jax.nn.sigmoid(a_gate)
    a_ext = jnp.concatenate([conv_hist.astype(dt), a], axis=1)
    conv = lax.conv_general_dilated(
        a_ext, conv_w[:, None, :].astype(dt), window_strides=(1,), padding='VALID',
        dimension_numbers=('NWC', 'WIO', 'NWC'), feature_group_count=D_CONV) + conv_b
    cf = conv.astype(jnp.float32)
    mu = jnp.mean(cf, axis=-1, keepdims=True)
    var = jnp.mean(jnp.square(cf - mu), axis=-1, keepdims=True)
    cn = ((cf - mu) * lax.rsqrt(var + EPS) * ln_g + ln_b).astype(dt)
    o_a = jax.nn.silu(cn) @ w_conv_out

    p_hist = MAX_POOL - 1
    u_ext = jnp.concatenate([pool_hist.astype(dt), u], axis=1)
    s = jnp.cumsum(u_ext.astype(jnp.float32), axis=1)
    s = jnp.concatenate([jnp.zeros((n, 1, D_POOL), jnp.float32), s], axis=1)
    pos = pos0 + jnp.arange(t)
    outs = []
    for gi, w in enumerate(POOL_WINDOWS):
        sl = slice(gi * POOL_GROUP, (gi + 1) * POOL_GROUP)
        win_sum = s[:, p_hist + 1:p_hist + 1 + t, sl] - s[:, p_hist + 1 - w:p_hist + 1 - w + t, sl]
        cnt = jnp.minimum(pos + 1, w).astype(jnp.float32)[None, :, None]
        outs.append(win_sum / cnt - u[:, :, sl].astype(jnp.float32))
    pooled = jnp.concatenate(outs, axis=-1).astype(dt)
    pg = pooled.reshape(n, t, N_POOL_GROUPS, POOL_GROUP)
    pm = jnp.einsum('ntgc,gcd->ntgd', pg, pool_mix).reshape(n, t, D_POOL) * pool_scale
    o_b = pm @ w_pool_out

    m = jax.nn.sigmoid(ga) * o_a + jax.nn.sigmoid(gb) * o_b
    x = x + (g1 * (m @ w_out)).astype(dt)

    h2 = (_rms(x) * (1.0 + sc2) + sh2).astype(dt)
    f = jnp.square(jax.nn.relu(h2 @ w_ff1)) @ w_ff2
    x = x + (g2 * f).astype(dt)
    return x, a_ext[:, -(CONV_W - 1):], u_ext[:, -(MAX_POOL - 1):]


def setup_inputs(seed: int = 0) -> dict:
    key = jax.random.key(seed)
    ks = jax.random.split(key, 24)
    f32 = jnp.float32
    nrm = lambda k, shape, s: jax.random.normal(k, shape, f32) * s
    return {
        "x_prompt": nrm(ks[0], (BATCH, SEQ, D_MODEL), 1.0),
        "x_sample": nrm(ks[1], (DEC_BATCH, DEC_SEQ, D_MODEL), 1.0),
        "state_conv": nrm(ks[2], (DEPTH, DEC_BATCH, CONV_W - 1, D_CONV), 1.0),
        "state_pool": nrm(ks[3], (DEPTH, DEC_BATCH, MAX_POOL - 1, D_POOL), 1.0),
        "c_prompt": nrm(ks[4], (BATCH, D_MODEL), 1.0),
        "c_sample": nrm(ks[5], (DEC_BATCH, D_MODEL), 1.0),
        "w_ada": nrm(ks[6], (DEPTH, D_MODEL, N_MOD * D_MODEL), 0.5 * D_MODEL ** -0.5),
        "b_ada": nrm(ks[7], (DEPTH, N_MOD * D_MODEL), 0.01),
        "w_in": nrm(ks[8], (DEPTH, D_MODEL, D_IN), D_MODEL ** -0.5),
        "conv_w": nrm(ks[9], (DEPTH, CONV_W, D_CONV), CONV_W ** -0.5),
        "conv_b": nrm(ks[10], (DEPTH, D_CONV), 0.01),
        "ln_g": 1.0 + nrm(ks[11], (DEPTH, D_CONV), 0.05),
        "ln_b": nrm(ks[12], (DEPTH, D_CONV), 0.01),
        "w_conv_out": nrm(ks[13], (DEPTH, D_CONV, D_MODEL), D_CONV ** -0.5),
        "pool_mix": nrm(ks[14], (DEPTH, N_POOL_GROUPS, POOL_GROUP, POOL_GROUP), POOL_GROUP ** -0.5),
        "pool_scale": 1.0 + nrm(ks[15], (DEPTH, D_POOL), 0.05),
        "w_pool_out": nrm(ks[16], (DEPTH, D_POOL, D_MODEL), D_POOL ** -0.5),
        "w_out": nrm(ks[17], (DEPTH, D_MODEL, D_MODEL), D_MODEL ** -0.5),
        "w_ff1": nrm(ks[18], (DEPTH, D_MODEL, D_FF), D_MODEL ** -0.5),
        "w_ff2": nrm(ks[19], (DEPTH, D_FF, D_MODEL), D_FF ** -0.5),
        "final_g": 1.0 + nrm(ks[20], (D_MODEL,), 0.05),
    }


def reference(x_prompt, x_sample, state_conv, state_pool, c_prompt, c_sample, w_ada, b_ada, w_in,
              conv_w, conv_b, ln_g, ln_b, w_conv_out, pool_mix, pool_scale, w_pool_out, w_out,
              w_ff1, w_ff2, final_g):
    dt = x_prompt.dtype
    xp, xs = x_prompt, x_sample
    zc = jnp.zeros((BATCH, CONV_W - 1, D_CONV), dt)
    zp = jnp.zeros((BATCH, MAX_POOL - 1, D_POOL), dt)
    cp_l, pp_l, cs_l, ps_l = [], [], [], []
    for l in range(DEPTH):
        w = (w_ada[l], b_ada[l], w_in[l], conv_w[l], conv_b[l], ln_g[l], ln_b[l], w_conv_out[l],
             pool_mix[l], pool_scale[l], w_pool_out[l], w_out[l], w_ff1[l], w_ff2[l])
        xp, cp, pp = _layer(xp, c_prompt, zc, zp, 0, *w)
        xs, cs, ps = _layer(xs, c_sample, state_conv[l], state_pool[l], PAST_LEN, *w)
        cp_l.append(cp); pp_l.append(pp); cs_l.append(cs); ps_l.append(ps)
    y_prompt = (_rms(xp) * final_g).astype(dt)
    y_sample = (_rms(xs) * final_g).astype(dt)
    new_conv_prompt = jnp.stack(cp_l)
    new_pool_prompt = jnp.stack(pp_l)
    new_conv_sample = jnp.stack(cs_l)
    new_pool_sample = jnp.stack(ps_l)
    return (y_prompt, y_sample, new_conv_prompt, new_pool_prompt, new_conv_sample, new_pool_sample)
```

```python
import functools

import jax
import jax.numpy as jnp
from jax import lax
from jax.experimental import pallas as pl
from jax.experimental.pallas import tpu as pltpu

D = 1024
BATCH = 8
SEQ = 2048
DEC_BATCH = 128
DEC_SEQ = 4
PAST_LEN = 16384
CONV_W = 31
POOL_WINDOWS = (2, 4, 8, 16)
POOL_GROUP = D // len(POOL_WINDOWS)
MAX_POOL = max(POOL_WINDOWS)
D_FF = 4 * D
N_MOD = 6
EPS = 1e-6

SUBLANES = 8
LANES = 128
N_LANE_GROUPS = D // LANES
ROW_CHUNK = 32
COL_CHUNK = 256
TT = 32
TQ = TT * BATCH
CONV_HIST = 32
POOL_HIST = 16
VMEM_LIMIT = 58 * 1024 * 1024

f32 = jnp.float32
bf16 = jnp.bfloat16


def _dot(a, b):
    return jnp.dot(a, b, preferred_element_type=f32)


def _sigmoid(x):
    return 1.0 / (1.0 + jnp.exp(-x))


def _rms(x):
    return x * lax.rsqrt(jnp.mean(x * x, axis=-1, keepdims=True) + EPS)


def _const_spec(shape):
    nd = len(shape)
    return pl.BlockSpec(shape, lambda *_: (0,) * nd, pipeline_mode=pl.Buffered(1))


def _ada_kernel(cs_ref, cp_ref, w_ref, b_ref, o_ref):
    w = w_ref[...].astype(bf16)
    b = b_ref[0]
    cs = cs_ref[...]
    cp = cp_ref[...]
    o_ref[0, 0:DEC_BATCH, :] = _dot((cs * _sigmoid(cs)).astype(bf16), w) + b
    o_ref[0, DEC_BATCH:DEC_BATCH + BATCH, :] = _dot((cp * _sigmoid(cp)).astype(bf16), w) + b


def _ada(c_sample, c_prompt, w_ada, b_ada):
    return pl.pallas_call(
        _ada_kernel,
        grid=(N_MOD,),
        in_specs=[
            pl.BlockSpec((DEC_BATCH, D), lambda i: (0, 0)),
            pl.BlockSpec((BATCH, D), lambda i: (0, 0)),
            pl.BlockSpec((D, D), lambda i: (0, i)),
            pl.BlockSpec((1, 1, D), lambda i: (i, 0, 0)),
        ],
        out_specs=pl.BlockSpec((1, DEC_BATCH + BATCH, D), lambda i: (i, 0, 0)),
        out_shape=jax.ShapeDtypeStruct((N_MOD, DEC_BATCH + BATCH, D), f32),
        name="ada_mod",
    )(c_sample, c_prompt, w_ada, b_ada.reshape(N_MOD, 1, D))


def _mod_rows(mod_ref, i, r0, rb):
    if rb < ROW_CHUNK:
        m = mod_ref[i]
        return jnp.concatenate([m] * (ROW_CHUNK // rb), axis=0)
    return mod_ref[i, pl.ds(r0, ROW_CHUNK), :]


def _mod_cols(mod_ref, i, c0, rows, rb):
    m = mod_ref[i, :, c0:c0 + COL_CHUNK]
    if rb < rows:
        return jnp.concatenate([m] * (rows // rb), axis=0)
    return m


def _glu_and_pool_in(h_s, w_in, a_dst, u_dst):
    h = h_s[...]
    for c0 in range(0, D, COL_CHUNK):
        val = _dot(h, w_in[:, c0:c0 + COL_CHUNK])
        gate = _dot(h, w_in[:, D + c0:D + c0 + COL_CHUNK])
        a_dst[:, c0:c0 + COL_CHUNK] = val * _sigmoid(gate)
        u_dst[:, c0:c0 + COL_CHUNK] = _dot(h, w_in[:, 2 * D + c0:2 * D + c0 + COL_CHUNK])


def _ln_silu(cf, lg_ref, lb_ref):
    mu = jnp.mean(cf, axis=-1, keepdims=True)
    cc = cf - mu
    var = jnp.mean(cc * cc, axis=-1, keepdims=True)
    cn = cc * lax.rsqrt(var + EPS) * lg_ref[...] + lb_ref[...]
    return cn * _sigmoid(cn)


def _merge_and_mlp(rows, rb, load_x_cols, mod_ref, h_s, cn_s, pl_s, pm_s, m_s, x1_s, acc_s, hid_s,
                   w_in, w_co, pmix, w_po, w_out, w_f1, w_f2, ps_ref, store_y_rows, fg_ref):
    for g in range(len(POOL_WINDOWS)):
        c0 = g * POOL_GROUP
        pm = _dot(pl_s[:, c0:c0 + POOL_GROUP], pmix[g]) * ps_ref[:, c0:c0 + POOL_GROUP]
        pm_s[:, c0:c0 + POOL_GROUP] = pm.astype(bf16)
    h = h_s[...]
    cn = cn_s[...]
    pm = pm_s[...]
    for c0 in range(0, D, COL_CHUNK):
        ga = _dot(h, w_in[:, 3 * D + c0:3 * D + c0 + COL_CHUNK])
        gb = _dot(h, w_in[:, 4 * D + c0:4 * D + c0 + COL_CHUNK])
        o_a = _dot(cn, w_co[:, c0:c0 + COL_CHUNK])
        o_b = _dot(pm, w_po[:, c0:c0 + COL_CHUNK])
        m_s[:, c0:c0 + COL_CHUNK] = (_sigmoid(ga) * o_a + _sigmoid(gb) * o_b).astype(bf16)
    m = m_s[...]
    for c0 in range(0, D, COL_CHUNK):
        g1 = _mod_cols(mod_ref, 2, c0, rows, rb)
        x1_s[:, c0:c0 + COL_CHUNK] = load_x_cols(c0) + g1 * _dot(m, w_out[:, c0:c0 + COL_CHUNK])

    def norm2(c, carry):
        r0 = pl.multiple_of(c * ROW_CHUNK, ROW_CHUNK)
        x1 = x1_s[pl.ds(r0, ROW_CHUNK), :]
        sh2 = _mod_rows(mod_ref, 3, r0, rb)
        sc2 = _mod_rows(mod_ref, 4, r0, rb)
        h_s[pl.ds(r0, ROW_CHUNK), :] = (_rms(x1) * (1.0 + sc2) + sh2).astype(bf16)
        return carry
    lax.fori_loop(0, rows // ROW_CHUNK, norm2, 0)

    h2 = h_s[...]
    for f0 in range(0, D_FF, D):
        for c0 in range(0, D, COL_CHUNK):
            hid = jnp.maximum(_dot(h2, w_f1[:, f0 + c0:f0 + c0 + COL_CHUNK]), 0.0)
            hid_s[:, c0:c0 + COL_CHUNK] = (hid * hid).astype(bf16)
        hid = hid_s[...]
        for c0 in range(0, D, COL_CHUNK):
            part = _dot(hid, w_f2[f0:f0 + D, c0:c0 + COL_CHUNK])
            if f0 == 0:
                acc_s[:, c0:c0 + COL_CHUNK] = part
            else:
                acc_s[:, c0:c0 + COL_CHUNK] += part

    def final(c, carry):
        r0 = pl.multiple_of(c * ROW_CHUNK, ROW_CHUNK)
        g2 = _mod_rows(mod_ref, 5, r0, rb)
        x2 = x1_s[pl.ds(r0, ROW_CHUNK), :] + g2 * acc_s[pl.ds(r0, ROW_CHUNK), :]
        store_y_rows(r0, _rms(x2) * fg_ref[...])
        return carry
    lax.fori_loop(0, rows // ROW_CHUNK, final, 0)


def _prompt_kernel(x_ref, mod_ref, w_in, w_co, pmix, w_po, w_out, w_f1, w_f2,
                   cw_ref, cb_ref, lg_ref, lb_ref, ps_ref, fg_ref,
                   y_ref, ncp_ref, npp_ref,
                   xrow, yrow, h_s, a_ext, u_ext, cn_s, pl_s, pm_s, m_s, x1_s, acc_s, hid_s):
    j = pl.program_id(0)
    n_steps = pl.num_programs(0)
    a_h = CONV_HIST * BATCH
    u_h = POOL_HIST * BATCH

    @pl.when(j == 0)
    def _():
        a_ext[0:a_h, :] = jnp.zeros((a_h, D), f32)
        u_ext[0:u_h, :] = jnp.zeros((u_h, D), f32)

    for b in range(BATCH):
        for t8 in range(0, TT, SUBLANES):
            for g in range(N_LANE_GROUPS):
                xrow[g, pl.ds(t8 * BATCH + b, SUBLANES, stride=BATCH), :] = (
                    x_ref[b, t8:t8 + SUBLANES, g * LANES:(g + 1) * LANES])

    def load_x_rows(r0):
        return jnp.concatenate([xrow[g, pl.ds(r0, ROW_CHUNK), :] for g in range(N_LANE_GROUPS)], axis=1)

    def load_x_cols(c0):
        g0 = c0 // LANES
        return jnp.concatenate([xrow[g0 + i] for i in range(COL_CHUNK // LANES)], axis=1)

    def norm1(c, carry):
        r0 = pl.multiple_of(c * ROW_CHUNK, ROW_CHUNK)
        x = load_x_rows(r0)
        sh1 = _mod_rows(mod_ref, 0, r0, BATCH)
        sc1 = _mod_rows(mod_ref, 1, r0, BATCH)
        h_s[pl.ds(r0, ROW_CHUNK), :] = (_rms(x) * (1.0 + sc1) + sh1).astype(bf16)
        return carry
    lax.fori_loop(0, TQ // ROW_CHUNK, norm1, 0)

    _glu_and_pool_in(h_s, w_in, a_ext.at[a_h:a_h + TQ], u_ext.at[u_h:u_h + TQ])

    slabs = ROW_CHUNK // BATCH

    def conv_pool(c, carry):
        r0 = pl.multiple_of(c * ROW_CHUNK, ROW_CHUNK)
        first = a_h - (CONV_W - 1) * BATCH
        parts = []
        for c0 in range(0, D, COL_CHUNK):
            acc = jnp.zeros((ROW_CHUNK, COL_CHUNK), f32)
            for k in range(CONV_W):
                rk = pl.multiple_of(r0 + first + k * BATCH, BATCH)
                acc = acc + a_ext[pl.ds(rk, ROW_CHUNK), c0:c0 + COL_CHUNK] * cw_ref[k:k + 1, c0:c0 + COL_CHUNK]
            parts.append(acc)
        cf = jnp.concatenate(parts, axis=1) + cb_ref[...]
        cn_s[pl.ds(r0, ROW_CHUNK), :] = _ln_silu(cf, lg_ref, lb_ref).astype(bf16)
        t_idx = j * TT + c * slabs + lax.broadcasted_iota(jnp.int32, (ROW_CHUNK, 1), 0) // BATCH
        parts = []
        for gi, w in enumerate(POOL_WINDOWS):
            c0 = gi * POOL_GROUP
            acc = jnp.zeros((ROW_CHUNK, POOL_GROUP), f32)
            for jj in range(w):
                rk = pl.multiple_of(r0 + u_h - jj * BATCH, BATCH)
                acc = acc + u_ext[pl.ds(rk, ROW_CHUNK), c0:c0 + POOL_GROUP]
            inv_cnt = 1.0 / jnp.minimum(t_idx + 1, w).astype(f32)
            parts.append(acc * inv_cnt - u_ext[pl.ds(pl.multiple_of(r0 + u_h, BATCH), ROW_CHUNK), c0:c0 + POOL_GROUP])
        pl_s[pl.ds(r0, ROW_CHUNK), :] = jnp.concatenate(parts, axis=1).astype(bf16)
        return carry
    lax.fori_loop(0, TQ // ROW_CHUNK, conv_pool, 0)

    def store_y_rows(r0, y):
        for g in range(N_LANE_GROUPS):
            yrow[g, pl.ds(r0, ROW_CHUNK), :] = y[:, g * LANES:(g + 1) * LANES]

    _merge_and_mlp(TQ, BATCH, load_x_cols, mod_ref, h_s, cn_s, pl_s, pm_s, m_s, x1_s, acc_s, hid_s,
                   w_in, w_co, pmix, w_po, w_out, w_f1, w_f2, ps_ref, store_y_rows, fg_ref)

    def deinterleave(dst_ref, n_t):
        for b in range(BATCH):
            for t8 in range(0, n_t, SUBLANES):
                for g in range(N_LANE_GROUPS):
                    dst_ref[b, t8:t8 + SUBLANES, g * LANES:(g + 1) * LANES] = (
                        yrow[g, pl.ds(t8 * BATCH + b, SUBLANES, stride=BATCH), :])

    deinterleave(y_ref, TT)

    a_ext[0:a_h, :] = a_ext[TQ:TQ + a_h, :]
    u_ext[0:u_h, :] = u_ext[TQ:TQ + u_h, :]

    @pl.when(j == n_steps - 1)
    def _():
        for g in range(N_LANE_GROUPS):
            yrow[g, 0:a_h, :] = a_ext[0:a_h, g * LANES:(g + 1) * LANES]
        deinterleave(ncp_ref, CONV_HIST)
        for g in range(N_LANE_GROUPS):
            yrow[g, 0:u_h, :] = u_ext[0:u_h, g * LANES:(g + 1) * LANES]
        deinterleave(npp_ref, POOL_HIST)


def _weight_specs():
    return [
        _const_spec((D, 5 * D)),
        _const_spec((D, D)),
        _const_spec((len(POOL_WINDOWS), POOL_GROUP, POOL_GROUP)),
        _const_spec((D, D)),
        _const_spec((D, D)),
        _const_spec((D, D_FF)),
        _const_spec((D_FF, D)),
        _const_spec((CONV_W, D)),
        _const_spec((1, D)),
        _const_spec((1, D)),
        _const_spec((1, D)),
        _const_spec((1, D)),
        _const_spec((1, D)),
    ]


def _prompt_layer(x_prompt, mod, weights):
    n_steps = SEQ // TT
    return pl.pallas_call(
        _prompt_kernel,
        grid=(n_steps,),
        in_specs=[
            pl.BlockSpec((BATCH, TT, D), lambda j: (0, j, 0)),
            pl.BlockSpec((N_MOD, BATCH, D), lambda j: (0, DEC_BATCH // BATCH, 0)),
        ] + _weight_specs(),
        out_specs=[
            pl.BlockSpec((BATCH, TT, D), lambda j: (0, j, 0)),
            pl.BlockSpec((BATCH, CONV_HIST, D), lambda j: (0, 0, 0)),
            pl.BlockSpec((BATCH, POOL_HIST, D), lambda j: (0, 0, 0)),
        ],
        out_shape=[
            jax.ShapeDtypeStruct((BATCH, SEQ, D), f32),
            jax.ShapeDtypeStruct((BATCH, CONV_HIST, D), f32),
            jax.ShapeDtypeStruct((BATCH, POOL_HIST, D), f32),
        ],
        scratch_shapes=[
            pltpu.VMEM((N_LANE_GROUPS, TQ, LANES), f32),
            pltpu.VMEM((N_LANE_GROUPS, TQ, LANES), f32),
            pltpu.VMEM((TQ, D), bf16),
            pltpu.VMEM(((CONV_HIST + TT) * BATCH, D), f32),
            pltpu.VMEM(((POOL_HIST + TT) * BATCH, D), f32),
            pltpu.VMEM((TQ, D), bf16),
            pltpu.VMEM((TQ, D), bf16),
            pltpu.VMEM((TQ, D), bf16),
            pltpu.VMEM((TQ, D), bf16),
            pltpu.VMEM((TQ, D), f32),
            pltpu.VMEM((TQ, D), f32),
            pltpu.VMEM((TQ, D), bf16),
        ],
        compiler_params=pltpu.CompilerParams(
            dimension_semantics=("arbitrary",), vmem_limit_bytes=VMEM_LIMIT),
        name="prompt_layer",
    )(x_prompt, mod, *weights)


HIST_ROWS = 16


def _hist_kernel(sc_ref, sp_ref, cw_ref, hc_ref, hp_ref):
    for t in range(DEC_SEQ):
        for c0 in range(0, D, COL_CHUNK):
            acc = jnp.zeros((HIST_ROWS, COL_CHUNK), f32)
            for jh in range(t, CONV_W - 1):
                acc = acc + sc_ref[jh, :, c0:c0 + COL_CHUNK] * cw_ref[jh - t:jh - t + 1, c0:c0 + COL_CHUNK]
            hc_ref[t, :, c0:c0 + COL_CHUNK] = acc
        for gi, w in enumerate(POOL_WINDOWS):
            c0 = gi * POOL_GROUP
            acc = jnp.zeros((HIST_ROWS, POOL_GROUP), f32)
            for jj in range(t + 1, w):
                acc = acc + sp_ref[MAX_POOL - 1 + t - jj, :, c0:c0 + POOL_GROUP]
            hp_ref[t, :, c0:c0 + POOL_GROUP] = acc


def _sample_hist(sc_t, sp_t, conv_w):
    return pl.pallas_call(
        _hist_kernel,
        grid=(DEC_BATCH // HIST_ROWS,),
        in_specs=[
            pl.BlockSpec((CONV_W - 1, HIST_ROWS, D), lambda i: (0, i, 0)),
            pl.BlockSpec((MAX_POOL - 1, HIST_ROWS, D), lambda i: (0, i, 0)),
            pl.BlockSpec((CONV_W, D), lambda i: (0, 0)),
        ],
        out_specs=[
            pl.BlockSpec((DEC_SEQ, HIST_ROWS, D), lambda i: (0, i, 0)),
            pl.BlockSpec((DEC_SEQ, HIST_ROWS, D), lambda i: (0, i, 0)),
        ],
        out_shape=[
            jax.ShapeDtypeStruct((DEC_SEQ, DEC_BATCH, D), f32),
            jax.ShapeDtypeStruct((DEC_SEQ, DEC_BATCH, D), f32),
        ],
        name="sample_hist",
    )(sc_t, sp_t, conv_w)


def _sample_kernel(x_ref, hc_ref, hp_ref, mod_ref, w_in, w_co, pmix, w_po, w_out, w_f1, w_f2,
                   cw_ref, cb_ref, lg_ref, lb_ref, ps_ref, fg_ref,
                   y_ref, an_ref, un_ref,
                   a_all, u_all, h_s, cn_s, pl_s, pm_s, m_s, x1_s, acc_s, hid_s):
    t = pl.program_id(0)
    rows = DEC_BATCH

    def norm1(c, carry):
        r0 = pl.multiple_of(c * ROW_CHUNK, ROW_CHUNK)
        x = x_ref[0, pl.ds(r0, ROW_CHUNK), :]
        sh1 = _mod_rows(mod_ref, 0, r0, rows)
        sc1 = _mod_rows(mod_ref, 1, r0, rows)
        h_s[pl.ds(r0, ROW_CHUNK), :] = (_rms(x) * (1.0 + sc1) + sh1).astype(bf16)
        return carry
    lax.fori_loop(0, rows // ROW_CHUNK, norm1, 0)

    _glu_and_pool_in(h_s, w_in, a_all.at[t], u_all.at[t])
    an_ref[0] = a_all[t]
    un_ref[0] = u_all[t]

    def conv_pool(c, carry):
        r0 = pl.multiple_of(c * ROW_CHUNK, ROW_CHUNK)
        cf = hc_ref[0, pl.ds(r0, ROW_CHUNK), :] + cb_ref[...]
        for d in range(DEC_SEQ):
            prev = a_all[jnp.maximum(t - d, 0), pl.ds(r0, ROW_CHUNK), :]
            k = CONV_W - 1 - d
            cf = cf + jnp.where(t >= d, prev * cw_ref[k:k + 1, :], 0.0)
        cn_s[pl.ds(r0, ROW_CHUNK), :] = _ln_silu(cf, lg_ref, lb_ref).astype(bf16)
        parts = []
        for gi, w in enumerate(POOL_WINDOWS):
            c0 = gi * POOL_GROUP
            acc = hp_ref[0, pl.ds(r0, ROW_CHUNK), c0:c0 + POOL_GROUP]
            for d in range(min(w, DEC_SEQ)):
                prev = u_all[jnp.maximum(t - d, 0), pl.ds(r0, ROW_CHUNK), c0:c0 + POOL_GROUP]
                acc = acc + jnp.where(t >= d, prev, 0.0)
            inv_cnt = 1.0 / jnp.minimum(PAST_LEN + t + 1, w).astype(f32)
            parts.append(acc * inv_cnt - u_all[t, pl.ds(r0, ROW_CHUNK), c0:c0 + POOL_GROUP])
        pl_s[pl.ds(r0, ROW_CHUNK), :] = jnp.concatenate(parts, axis=1).astype(bf16)
        return carry
    lax.fori_loop(0, rows // ROW_CHUNK, conv_pool, 0)

    def load_x_cols(c0):
        return x_ref[0, :, c0:c0 + COL_CHUNK]

    def store_y_rows(r0, y):
        y_ref[0, pl.ds(r0, ROW_CHUNK), :] = y

    _merge_and_mlp(rows, rows, load_x_cols, mod_ref, h_s, cn_s, pl_s, pm_s, m_s, x1_s, acc_s, hid_s,
                   w_in, w_co, pmix, w_po, w_out, w_f1, w_f2, ps_ref, store_y_rows, fg_ref)


def _sample_layer(xs_t, hc, hp, mod, weights):
    slab = lambda: pl.BlockSpec((1, DEC_BATCH, D), lambda t: (t, 0, 0))
    return pl.pallas_call(
        _sample_kernel,
        grid=(DEC_SEQ,),
        in_specs=[slab(), slab(), slab(),
                  pl.BlockSpec((N_MOD, DEC_BATCH, D), lambda t: (0, 0, 0))] + _weight_specs(),
        out_specs=[slab(), slab(), slab()],
        out_shape=[jax.ShapeDtypeStruct((DEC_SEQ, DEC_BATCH, D), f32)] * 3,
        scratch_shapes=[
            pltpu.VMEM((DEC_SEQ, DEC_BATCH, D), f32),
            pltpu.VMEM((DEC_SEQ, DEC_BATCH, D), f32),
            pltpu.VMEM((DEC_BATCH, D), bf16),
            pltpu.VMEM((DEC_BATCH, D), bf16),
            pltpu.VMEM((DEC_BATCH, D), bf16),
            pltpu.VMEM((DEC_BATCH, D), bf16),
            pltpu.VMEM((DEC_BATCH, D), bf16),
            pltpu.VMEM((DEC_BATCH, D), f32),
            pltpu.VMEM((DEC_BATCH, D), f32),
            pltpu.VMEM((DEC_BATCH, D), bf16),
        ],
        compiler_params=pltpu.CompilerParams(
            dimension_semantics=("arbitrary",), vmem_limit_bytes=VMEM_LIMIT),
        name="sample_layer",
    )(xs_t, hc, hp, mod, *weights)


def kernel(x_prompt, x_sample, state_conv, state_pool, c_prompt, c_sample, w_ada, b_ada, w_in,
           conv_w, conv_b, ln_g, ln_b, w_conv_out, pool_mix, pool_scale, w_pool_out, w_out,
           w_ff1, w_ff2, final_g):
    assert w_in.shape[0] == 1, "single layer"
    mod = _ada(c_sample, c_prompt, w_ada[0], b_ada[0])
    weights = (
        w_in[0].astype(bf16), w_conv_out[0].astype(bf16), pool_mix[0].astype(bf16),
        w_pool_out[0].astype(bf16), w_out[0].astype(bf16), w_ff1[0].astype(bf16), w_ff2[0].astype(bf16),
        conv_w[0], conv_b, ln_g, ln_b, pool_scale, final_g.reshape(1, D),
    )
    y_prompt, ncp, npp = _prompt_layer(x_prompt, mod, weights)
    new_conv_prompt = ncp[None, :, CONV_HIST - (CONV_W - 1):, :]
    new_pool_prompt = npp[None, :, POOL_HIST - (MAX_POOL - 1):, :]

    sc = state_conv[0]
    sp = state_pool[0]
    hc, hp = _sample_hist(sc.transpose(1, 0, 2), sp.transpose(1, 0, 2), conv_w[0])
    ys_t, a_new, u_new = _sample_layer(x_sample.transpose(1, 0, 2), hc, hp, mod, weights)
    y_sample = ys_t.transpose(1, 0, 2)
    new_conv_sample = jnp.concatenate([sc[:, DEC_SEQ:], a_new.transpose(1, 0, 2)], axis=1)[None]
    new_pool_sample = jnp.concatenate([sp[:, DEC_SEQ:], u_new.transpose(1, 0, 2)], axis=1)[None]
    return (y_prompt, y_sample, new_conv_prompt, new_pool_prompt, new_conv_sample, new_pool_sample)
```

```python
import jax
import jax.numpy as jnp
from jax import lax
from jax.experimental import pallas as pl
from jax.experimental.pallas import tpu as pltpu

D = 1024
BATCH = 8
SEQ = 2048
DEC_BATCH = 128
DEC_SEQ = 4
PAST_LEN = 16384
CONV_W = 31
POOL_WINDOWS = (2, 4, 8, 16)
POOL_GROUP = D // len(POOL_WINDOWS)
MAX_POOL = max(POOL_WINDOWS)
D_FF = 4 * D
N_MOD = 6
EPS = 1e-6

SUBLANES = 8
LANES = 128
N_LANE_GROUPS = D // LANES
ROW_CHUNK = 32
COL_CHUNK = 256
TT = 32
TQ = TT * BATCH
CONV_HIST = 32
POOL_HIST = 16
VMEM_LIMIT = 58 * 1024 * 1024

f32 = jnp.float32
bf16 = jnp.bfloat16


def _dot(a, b):
    return jnp.dot(a, b, preferred_element_type=f32)


def _sigmoid(x):
    return 1.0 / (1.0 + jnp.exp(-x))


def _rms(x):
    return x * lax.rsqrt(jnp.mean(x * x, axis=-1, keepdims=True) + EPS)


def _const_spec(shape):
    nd = len(shape)
    return pl.BlockSpec(shape, lambda *_: (0,) * nd, pipeline_mode=pl.Buffered(1))


def _ada_kernel(cs_ref, cp_ref, w_ref, b_ref, o_ref):
    w = w_ref[...].astype(bf16)
    b = b_ref[0]
    cs = cs_ref[...]
    cp = cp_ref[...]
    o_ref[0, 0:DEC_BATCH, :] = _dot((cs * _sigmoid(cs)).astype(bf16), w) + b
    o_ref[0, DEC_BATCH:DEC_BATCH + BATCH, :] = _dot((cp * _sigmoid(cp)).astype(bf16), w) + b


def _ada(c_sample, c_prompt, w_ada, b_ada):
    return pl.pallas_call(
        _ada_kernel,
        grid=(N_MOD,),
        in_specs=[
            pl.BlockSpec((DEC_BATCH, D), lambda i: (0, 0)),
            pl.BlockSpec((BATCH, D), lambda i: (0, 0)),
            pl.BlockSpec((D, D), lambda i: (0, i)),
            pl.BlockSpec((1, 1, D), lambda i: (i, 0, 0)),
        ],
        out_specs=pl.BlockSpec((1, DEC_BATCH + BATCH, D), lambda i: (i, 0, 0)),
        out_shape=jax.ShapeDtypeStruct((N_MOD, DEC_BATCH + BATCH, D), f32),
        name="ada_mod",
    )(c_sample, c_prompt, w_ada, b_ada.reshape(N_MOD, 1, D))


def _mod_rows(mod_ref, i, r0, rb):
    if rb < ROW_CHUNK:
        m = mod_ref[i]
        return jnp.concatenate([m] * (ROW_CHUNK // rb), axis=0)
    return mod_ref[i, r0:r0 + ROW_CHUNK, :]


def _mod_cols(mod_ref, i, c0, rows, rb):
    m = mod_ref[i, :, c0:c0 + COL_CHUNK]
    if rb < rows:
        return jnp.concatenate([m] * (rows // rb), axis=0)
    return m


def _glu_and_pool_in(h_s, w_in, a_dst, u_dst):
    h = h_s[...]
    for c0 in range(0, D, COL_CHUNK):
        val = _dot(h, w_in[:, c0:c0 + COL_CHUNK])
        gate = _dot(h, w_in[:, D + c0:D + c0 + COL_CHUNK])
        a_dst[:, c0:c0 + COL_CHUNK] = val * _sigmoid(gate)
        u_dst[:, c0:c0 + COL_CHUNK] = _dot(h, w_in[:, 2 * D + c0:2 * D + c0 + COL_CHUNK])


def _ln_silu(cf, lg_ref, lb_ref):
    mu = jnp.mean(cf, axis=-1, keepdims=True)
    cc = cf - mu
    var = jnp.mean(cc * cc, axis=-1, keepdims=True)
    cn = cc * lax.rsqrt(var + EPS) * lg_ref[...] + lb_ref[...]
    return cn * _sigmoid(cn)


def _norm_mod(load_rows, dst, mod_ref, i_shift, i_scale, rows, rb):
    for r0 in range(0, rows, ROW_CHUNK):
        x = load_rows(r0)
        shift = _mod_rows(mod_ref, i_shift, r0, rb)
        scale = _mod_rows(mod_ref, i_scale, r0, rb)
        dst[r0:r0 + ROW_CHUNK, :] = (_rms(x) * (1.0 + scale) + shift).astype(bf16)


def _merge(rows, rb, load_x_cols, mod_ref, h_s, h2_s, cn_s, pl_s, pm_s, m_s, x1_s,
           w_in, w_co, pmix, w_po, w_out, ps_ref):
    for g in range(len(POOL_WINDOWS)):
        c0 = g * POOL_GROUP
        pm = _dot(pl_s[:, c0:c0 + POOL_GROUP], pmix[g]) * ps_ref[:, c0:c0 + POOL_GROUP]
        pm_s[:, c0:c0 + POOL_GROUP] = pm.astype(bf16)
    h = h_s[...]
    cn = cn_s[...]
    pm = pm_s[...]
    for c0 in range(0, D, COL_CHUNK):
        ga = _dot(h, w_in[:, 3 * D + c0:3 * D + c0 + COL_CHUNK])
        gb = _dot(h, w_in[:, 4 * D + c0:4 * D + c0 + COL_CHUNK])
        o_a = _dot(cn, w_co[:, c0:c0 + COL_CHUNK])
        o_b = _dot(pm, w_po[:, c0:c0 + COL_CHUNK])
        m_s[:, c0:c0 + COL_CHUNK] = (_sigmoid(ga) * o_a + _sigmoid(gb) * o_b).astype(bf16)
    m = m_s[...]
    for c0 in range(0, D, COL_CHUNK):
        g1 = _mod_cols(mod_ref, 2, c0, rows, rb)
        x1_s[:, c0:c0 + COL_CHUNK] = load_x_cols(c0) + g1 * _dot(m, w_out[:, c0:c0 + COL_CHUNK])
    _norm_mod(lambda r0: x1_s[r0:r0 + ROW_CHUNK, :], h2_s, mod_ref, 3, 4, rows, rb)


def _mlp(h2_s, hid_s, acc_s, w_f1, w_f2):
    h2 = h2_s[...]
    for f0 in range(0, D_FF, D):
        for c0 in range(0, D, COL_CHUNK):
            hid = jnp.maximum(_dot(h2, w_f1[:, f0 + c0:f0 + c0 + COL_CHUNK]), 0.0)
            hid_s[:, c0:c0 + COL_CHUNK] = (hid * hid).astype(bf16)
        hid = hid_s[...]
        for c0 in range(0, D, COL_CHUNK):
            part = _dot(hid, w_f2[f0:f0 + D, c0:c0 + COL_CHUNK])
            if f0 == 0:
                acc_s[:, c0:c0 + COL_CHUNK] = part
            else:
                acc_s[:, c0:c0 + COL_CHUNK] += part


def _final(rows, rb, mod_ref, x1_s, acc_s, fg_ref, store_y_rows):
    for r0 in range(0, rows, ROW_CHUNK):
        g2 = _mod_rows(mod_ref, 5, r0, rb)
        x2 = x1_s[r0:r0 + ROW_CHUNK, :] + g2 * acc_s[r0:r0 + ROW_CHUNK, :]
        store_y_rows(r0, _rms(x2) * fg_ref[...])


def _prompt_kernel(x_ref, mod_ref, w_in, w_co, pmix, w_po, w_out, w_f1, w_f2,
                   cw_ref, cb_ref, lg_ref, lb_ref, ps_ref, fg_ref,
                   y_ref, ncp_ref, npp_ref,
                   xrow, yrow, h_s, h2_s, a_ext, u_ext, cn_s, pl_s, pm_s, m_s, x1_s, acc_s, hid_s):
    j = pl.program_id(0)
    n_tiles = pl.num_programs(0) - 1
    a_h = CONV_HIST * BATCH
    u_h = POOL_HIST * BATCH

    @pl.when(j == 0)
    def _():
        a_ext[0:a_h, :] = jnp.zeros((a_h, D), f32)
        u_ext[0:u_h, :] = jnp.zeros((u_h, D), f32)
        h2_s[...] = jnp.zeros((TQ, D), bf16)
        x1_s[...] = jnp.zeros((TQ, D), f32)

    for b in range(BATCH):
        for t8 in range(0, TT, SUBLANES):
            for g in range(N_LANE_GROUPS):
                xrow[g, pl.ds(t8 * BATCH + b, SUBLANES, stride=BATCH), :] = (
                    x_ref[b, t8:t8 + SUBLANES, g * LANES:(g + 1) * LANES])

    def load_x_rows(r0):
        return jnp.concatenate([xrow[g, r0:r0 + ROW_CHUNK, :] for g in range(N_LANE_GROUPS)], axis=1)

    def load_x_cols(c0):
        g0 = c0 // LANES
        return jnp.concatenate([xrow[g0 + i] for i in range(COL_CHUNK // LANES)], axis=1)

    _norm_mod(load_x_rows, h_s, mod_ref, 0, 1, TQ, BATCH)
    _glu_and_pool_in(h_s, w_in, a_ext.at[a_h:a_h + TQ], u_ext.at[u_h:u_h + TQ])

    _mlp(h2_s, hid_s, acc_s, w_f1, w_f2)

    first = a_h - (CONV_W - 1) * BATCH
    for r0 in range(0, TQ, ROW_CHUNK):
        parts = []
        for c0 in range(0, D, COL_CHUNK):
            acc = jnp.zeros((ROW_CHUNK, COL_CHUNK), f32)
            for k in range(CONV_W):
                rk = r0 + first + k * BATCH
                acc = acc + a_ext[rk:rk + ROW_CHUNK, c0:c0 + COL_CHUNK] * cw_ref[k:k + 1, c0:c0 + COL_CHUNK]
            parts.append(acc)
        cf = jnp.concatenate(parts, axis=1) + cb_ref[...]
        cn_s[r0:r0 + ROW_CHUNK, :] = _ln_silu(cf, lg_ref, lb_ref).astype(bf16)
        t_idx = j * TT + r0 // BATCH + lax.broadcasted_iota(jnp.int32, (ROW_CHUNK, 1), 0) // BATCH
        parts = []
        for gi, w in enumerate(POOL_WINDOWS):
            c0 = gi * POOL_GROUP
            acc = jnp.zeros((ROW_CHUNK, POOL_GROUP), f32)
            for jj in range(w):
                rk = r0 + u_h - jj * BATCH
                acc = acc + u_ext[rk:rk + ROW_CHUNK, c0:c0 + POOL_GROUP]
            inv_cnt = 1.0 / jnp.minimum(t_idx + 1, w).astype(f32)
            parts.append(acc * inv_cnt - u_ext[r0 + u_h:r0 + u_h + ROW_CHUNK, c0:c0 + POOL_GROUP])
        pl_s[r0:r0 + ROW_CHUNK, :] = jnp.concatenate(parts, axis=1).astype(bf16)

    def store_y_rows(r0, y):
        for g in range(N_LANE_GROUPS):
            yrow[g, r0:r0 + ROW_CHUNK, :] = y[:, g * LANES:(g + 1) * LANES]

    def deinterleave(dst_ref, n_t):
        for b in range(BATCH):
            for t8 in range(0, n_t, SUBLANES):
                for g in range(N_LANE_GROUPS):
                    dst_ref[b, t8:t8 + SUBLANES, g * LANES:(g + 1) * LANES] = (
                        yrow[g, pl.ds(t8 * BATCH + b, SUBLANES, stride=BATCH), :])

    _final(TQ, BATCH, mod_ref, x1_s, acc_s, fg_ref, store_y_rows)
    deinterleave(y_ref, TT)

    _merge(TQ, BATCH, load_x_cols, mod_ref, h_s, h2_s, cn_s, pl_s, pm_s, m_s, x1_s,
           w_in, w_co, pmix, w_po, w_out, ps_ref)

    a_ext[0:a_h, :] = a_ext[TQ:TQ + a_h, :]
    u_ext[0:u_h, :] = u_ext[TQ:TQ + u_h, :]

    @pl.when(j == n_tiles - 1)
    def _():
        for g in range(N_LANE_GROUPS):
            yrow[g, 0:a_h, :] = a_ext[0:a_h, g * LANES:(g + 1) * LANES]
        deinterleave(ncp_ref, CONV_HIST)
        for g in range(N_LANE_GROUPS):
            yrow[g, 0:u_h, :] = u_ext[0:u_h, g * LANES:(g + 1) * LANES]
        deinterleave(npp_ref, POOL_HIST)


def _weight_specs():
    return [
        _const_spec((D, 5 * D)),
        _const_spec((D, D)),
        _const_spec((len(POOL_WINDOWS), POOL_GROUP, POOL_GROUP)),
        _const_spec((D, D)),
        _const_spec((D, D)),
        _const_spec((D, D_FF)),
        _const_spec((D_FF, D)),
        _const_spec((CONV_W, D)),
        _const_spec((1, D)),
        _const_spec((1, D)),
        _const_spec((1, D)),
        _const_spec((1, D)),
        _const_spec((1, D)),
    ]


def _prompt_layer(x_prompt, mod, weights):
    n_tiles = SEQ // TT
    return pl.pallas_call(
        _prompt_kernel,
        grid=(n_tiles + 1,),
        in_specs=[
            pl.BlockSpec((BATCH, TT, D), lambda j: (0, jnp.minimum(j, n_tiles - 1), 0)),
            pl.BlockSpec((N_MOD, BATCH, D), lambda j: (0, DEC_BATCH // BATCH, 0)),
        ] + _weight_specs(),
        out_specs=[
            pl.BlockSpec((BATCH, TT, D), lambda j: (0, jnp.maximum(j - 1, 0), 0)),
            pl.BlockSpec((BATCH, CONV_HIST, D), lambda j: (0, 0, 0)),
            pl.BlockSpec((BATCH, POOL_HIST, D), lambda j: (0, 0, 0)),
        ],
        out_shape=[
            jax.ShapeDtypeStruct((BATCH, SEQ, D), f32),
            jax.ShapeDtypeStruct((BATCH, CONV_HIST, D), f32),
            jax.ShapeDtypeStruct((BATCH, POOL_HIST, D), f32),
        ],
        scratch_shapes=[
            pltpu.VMEM((N_LANE_GROUPS, TQ, LANES), f32),
            pltpu.VMEM((N_LANE_GROUPS, TQ, LANES), f32),
            pltpu.VMEM((TQ, D), bf16),
            pltpu.VMEM((TQ, D), bf16),
            pltpu.VMEM(((CONV_HIST + TT) * BATCH, D), f32),
            pltpu.VMEM(((POOL_HIST + TT) * BATCH, D), f32),
            pltpu.VMEM((TQ, D), bf16),
            pltpu.VMEM((TQ, D), bf16),
            pltpu.VMEM((TQ, D), bf16),
            pltpu.VMEM((TQ, D), bf16),
            pltpu.VMEM((TQ, D), f32),
            pltpu.VMEM((TQ, D), f32),
            pltpu.VMEM((TQ, D), bf16),
        ],
        compiler_params=pltpu.CompilerParams(
            dimension_semantics=("arbitrary",), vmem_limit_bytes=VMEM_LIMIT),
        name="prompt_layer",
    )(x_prompt, mod, *weights)


HIST_ROWS = 16


def _hist_kernel(sc_ref, sp_ref, cw_ref, hc_ref, hp_ref):
    for t in range(DEC_SEQ):
        for c0 in range(0, D, COL_CHUNK):
            acc = jnp.zeros((HIST_ROWS, COL_CHUNK), f32)
            for jh in range(t, CONV_W - 1):
                acc = acc + sc_ref[jh, :, c0:c0 + COL_CHUNK] * cw_ref[jh - t:jh - t + 1, c0:c0 + COL_CHUNK]
            hc_ref[t, :, c0:c0 + COL_CHUNK] = acc
        for gi, w in enumerate(POOL_WINDOWS):
            c0 = gi * POOL_GROUP
            acc = jnp.zeros((HIST_ROWS, POOL_GROUP), f32)
            for jj in range(t + 1, w):
                acc = acc + sp_ref[MAX_POOL - 1 + t - jj, :, c0:c0 + POOL_GROUP]
            hp_ref[t, :, c0:c0 + POOL_GROUP] = acc


def _sample_hist(sc_t, sp_t, conv_w):
    return pl.pallas_call(
        _hist_kernel,
        grid=(DEC_BATCH // HIST_ROWS,),
        in_specs=[
            pl.BlockSpec((CONV_W - 1, HIST_ROWS, D), lambda i: (0, i, 0)),
            pl.BlockSpec((MAX_POOL - 1, HIST_ROWS, D), lambda i: (0, i, 0)),
            pl.BlockSpec((CONV_W, D), lambda i: (0, 0)),
        ],
        out_specs=[
            pl.BlockSpec((DEC_SEQ, HIST_ROWS, D), lambda i: (0, i, 0)),
            pl.BlockSpec((DEC_SEQ, HIST_ROWS, D), lambda i: (0, i, 0)),
        ],
        out_shape=[
            jax.ShapeDtypeStruct((DEC_SEQ, DEC_BATCH, D), f32),
            jax.ShapeDtypeStruct((DEC_SEQ, DEC_BATCH, D), f32),
        ],
        name="sample_hist",
    )(sc_t, sp_t, conv_w)


def _sample_kernel(x_ref, hc_ref, hp_ref, mod_ref, w_in, w_co, pmix, w_po, w_out, w_f1, w_f2,
                   cw_ref, cb_ref, lg_ref, lb_ref, ps_ref, fg_ref,
                   y_ref, an_ref, un_ref,
                   a_all, u_all, h_s, h2_s, cn_s, pl_s, pm_s, m_s, x1_s, acc_s, hid_s):
    t = pl.program_id(0)
    rows = DEC_BATCH

    _norm_mod(lambda r0: x_ref[0, r0:r0 + ROW_CHUNK, :], h_s, mod_ref, 0, 1, rows, rows)
    _glu_and_pool_in(h_s, w_in, a_all.at[t], u_all.at[t])
    an_ref[0] = a_all[t]
    un_ref[0] = u_all[t]

    for r0 in range(0, rows, ROW_CHUNK):
        cf = hc_ref[0, r0:r0 + ROW_CHUNK, :] + cb_ref[...]
        for d in range(DEC_SEQ):
            prev = a_all[jnp.maximum(t - d, 0), r0:r0 + ROW_CHUNK, :]
            k = CONV_W - 1 - d
            cf = cf + jnp.where(t >= d, prev * cw_ref[k:k + 1, :], 0.0)
        cn_s[r0:r0 + ROW_CHUNK, :] = _ln_silu(cf, lg_ref, lb_ref).astype(bf16)
        parts = []
        for gi, w in enumerate(POOL_WINDOWS):
            c0 = gi * POOL_GROUP
            acc = hp_ref[0, r0:r0 + ROW_CHUNK, c0:c0 + POOL_GROUP]
            for d in range(min(w, DEC_SEQ)):
                prev = u_all[jnp.maximum(t - d, 0), r0:r0 + ROW_CHUNK, c0:c0 + POOL_GROUP]
                acc = acc + jnp.where(t >= d, prev, 0.0)
            cnt = jnp.minimum(jnp.full((ROW_CHUNK, 1), PAST_LEN + 1, jnp.int32) + t, w).astype(f32)
            parts.append(acc * (1.0 / cnt) - u_all[t, r0:r0 + ROW_CHUNK, c0:c0 + POOL_GROUP])
        pl_s[r0:r0 + ROW_CHUNK, :] = jnp.concatenate(parts, axis=1).astype(bf16)

    def load_x_cols(c0):
        return x_ref[0, :, c0:c0 + COL_CHUNK]

    def store_y_rows(r0, y):
        y_ref[0, r0:r0 + ROW_CHUNK, :] = y

    _merge(rows, rows, load_x_cols, mod_ref, h_s, h2_s, cn_s, pl_s, pm_s, m_s, x1_s,
           w_in, w_co, pmix, w_po, w_out, ps_ref)
    _mlp(h2_s, hid_s, acc_s, w_f1, w_f2)
    _final(rows, rows, mod_ref, x1_s, acc_s, fg_ref, store_y_rows)


def _sample_layer(xs_t, hc, hp, mod, weights):
    slab = lambda: pl.BlockSpec((1, DEC_BATCH, D), lambda t: (t, 0, 0))
    return pl.pallas_call(
        _sample_kernel,
        grid=(DEC_SEQ,),
        in_specs=[slab(), slab(), slab(),
                  pl.BlockSpec((N_MOD, DEC_BATCH, D), lambda t: (0, 0, 0))] + _weight_specs(),
        out_specs=[slab(), slab(), slab()],
        out_shape=[jax.ShapeDtypeStruct((DEC_SEQ, DEC_BATCH, D), f32)] * 3,
        scratch_shapes=[
            pltpu.VMEM((DEC_SEQ, DEC_BATCH, D), f32),
            pltpu.VMEM((DEC_SEQ, DEC_BATCH, D), f32),
            pltpu.VMEM((DEC_BATCH, D), bf16),
            pltpu.VMEM((DEC_BATCH, D), bf16),
            pltpu.VMEM((DEC_BATCH, D), bf16),
            pltpu.VMEM((DEC_BATCH, D), bf16),
            pltpu.VMEM((DEC_BATCH, D), bf16),
            pltpu.VMEM((DEC_BATCH, D), bf16),
            pltpu.VMEM((DEC_BATCH, D), f32),
            pltpu.VMEM((DEC_BATCH, D), f32),
            pltpu.VMEM((DEC_BATCH, D), bf16),
        ],
        compiler_params=pltpu.CompilerParams(
            dimension_semantics=("arbitrary",), vmem_limit_bytes=VMEM_LIMIT),
        name="sample_layer",
    )(xs_t, hc, hp, mod, *weights)


def kernel(x_prompt, x_sample, state_conv, state_pool, c_prompt, c_sample, w_ada, b_ada, w_in,
           conv_w, conv_b, ln_g, ln_b, w_conv_out, pool_mix, pool_scale, w_pool_out, w_out,
           w_ff1, w_ff2, final_g):
    assert w_in.shape[0] == 1, "single layer"
    mod = _ada(c_sample, c_prompt, w_ada[0], b_ada[0])
    weights = (
        w_in[0].astype(bf16), w_conv_out[0].astype(bf16), pool_mix[0].astype(bf16),
        w_pool_out[0].astype(bf16), w_out[0].astype(bf16), w_ff1[0].astype(bf16), w_ff2[0].astype(bf16),
        conv_w[0], conv_b, ln_g, ln_b, pool_scale, final_g.reshape(1, D),
    )
    y_prompt, ncp, npp = _prompt_layer(x_prompt, mod, weights)
    new_conv_prompt = ncp[None, :, CONV_HIST - (CONV_W - 1):, :]
    new_pool_prompt = npp[None, :, POOL_HIST - (MAX_POOL - 1):, :]

    sc = state_conv[0]
    sp = state_pool[0]
    hc, hp = _sample_hist(sc.transpose(1, 0, 2), sp.transpose(1, 0, 2), conv_w[0])
    ys_t, a_new, u_new = _sample_layer(x_sample.transpose(1, 0, 2), hc, hp, mod, weights)
    y_sample = ys_t.transpose(1, 0, 2)
    new_conv_sample = jnp.concatenate([sc[:, DEC_SEQ:], a_new.transpose(1, 0, 2)], axis=1)[None]
    new_pool_sample = jnp.concatenate([sp[:, DEC_SEQ:], u_new.transpose(1, 0, 2)], axis=1)[None]
    return (y_prompt, y_sample, new_conv_prompt, new_pool_prompt, new_conv_sample, new_pool_sample)
```

```python
import jax
import jax.numpy as jnp
from jax import lax
from jax.experimental import pallas as pl
from jax.experimental.pallas import tpu as pltpu

D = 1024
BATCH = 8
SEQ = 2048
DEC_BATCH = 128
DEC_SEQ = 4
PAST_LEN = 16384
CONV_W = 31
POOL_WINDOWS = (2, 4, 8, 16)
POOL_GROUP = D // len(POOL_WINDOWS)
MAX_POOL = max(POOL_WINDOWS)
D_FF = 4 * D
N_MOD = 6
EPS = 1e-6

SUBLANES = 8
LANES = 128
N_LANE_GROUPS = D // LANES
ROW_CHUNK = 32
COL_CHUNK = 256
TT = 32
TQ = TT * BATCH
CONV_HIST = 32
POOL_HIST = 16
VMEM_LIMIT = 58 * 1024 * 1024

f32 = jnp.float32
bf16 = jnp.bfloat16


def _dot(a, b):
    return jnp.dot(a, b, preferred_element_type=f32)


def _sigmoid(x):
    return 1.0 / (1.0 + jnp.exp(-x))


def _rms(x):
    return x * lax.rsqrt(jnp.mean(x * x, axis=-1, keepdims=True) + EPS)


def _const_spec(shape):
    nd = len(shape)
    return pl.BlockSpec(shape, lambda *_: (0,) * nd, pipeline_mode=pl.Buffered(1))


def _ada_kernel(cs_ref, cp_ref, w_ref, b_ref, o_ref):
    w = w_ref[...].astype(bf16)
    b = b_ref[0]
    cs = cs_ref[...]
    cp = cp_ref[...]
    o_ref[0, 0:DEC_BATCH, :] = _dot((cs * _sigmoid(cs)).astype(bf16), w) + b
    o_ref[0, DEC_BATCH:DEC_BATCH + BATCH, :] = _dot((cp * _sigmoid(cp)).astype(bf16), w) + b


def _ada(c_sample, c_prompt, w_ada, b_ada):
    return pl.pallas_call(
        _ada_kernel,
        grid=(N_MOD,),
        in_specs=[
            pl.BlockSpec((DEC_BATCH, D), lambda i: (0, 0)),
            pl.BlockSpec((BATCH, D), lambda i: (0, 0)),
            pl.BlockSpec((D, D), lambda i: (0, i)),
            pl.BlockSpec((1, 1, D), lambda i: (i, 0, 0)),
        ],
        out_specs=pl.BlockSpec((1, DEC_BATCH + BATCH, D), lambda i: (i, 0, 0)),
        out_shape=jax.ShapeDtypeStruct((N_MOD, DEC_BATCH + BATCH, D), f32),
        name="ada_mod",
    )(c_sample, c_prompt, w_ada, b_ada.reshape(N_MOD, 1, D))


def _mod_rows(mod_ref, i, r0, rb):
    if rb < ROW_CHUNK:
        m = mod_ref[i]
        return jnp.concatenate([m] * (ROW_CHUNK // rb), axis=0)
    return mod_ref[i, r0:r0 + ROW_CHUNK, :]


def _mod_cols(mod_ref, i, c0, rows, rb):
    m = mod_ref[i, :, c0:c0 + COL_CHUNK]
    if rb < rows:
        return jnp.concatenate([m] * (rows // rb), axis=0)
    return m


def _glu_and_pool_in(h_s, w_in, a_dst, u_dst):
    h = h_s[...]
    for c0 in range(0, D, COL_CHUNK):
        val = _dot(h, w_in[:, c0:c0 + COL_CHUNK])
        gate = _dot(h, w_in[:, D + c0:D + c0 + COL_CHUNK])
        a_dst[:, c0:c0 + COL_CHUNK] = val * _sigmoid(gate)
        u_dst[:, c0:c0 + COL_CHUNK] = _dot(h, w_in[:, 2 * D + c0:2 * D + c0 + COL_CHUNK])


def _ln_silu(cf, lg_ref, lb_ref):
    mu = jnp.mean(cf, axis=-1, keepdims=True)
    cc = cf - mu
    var = jnp.mean(cc * cc, axis=-1, keepdims=True)
    cn = cc * lax.rsqrt(var + EPS) * lg_ref[...] + lb_ref[...]
    return cn * _sigmoid(cn)


def _norm_mod(load_rows, dst, mod_ref, i_shift, i_scale, rows, rb):
    for r0 in range(0, rows, ROW_CHUNK):
        x = load_rows(r0)
        shift = _mod_rows(mod_ref, i_shift, r0, rb)
        scale = _mod_rows(mod_ref, i_scale, r0, rb)
        dst[r0:r0 + ROW_CHUNK, :] = (_rms(x) * (1.0 + scale) + shift).astype(bf16)


def _merge(rows, rb, load_x_cols, mod_ref, h_s, h2_s, cn_s, pl_s, pm_s, m_s, x1_s,
           w_in, w_co, pmix, w_po, w_out, ps_ref):
    for g in range(len(POOL_WINDOWS)):
        c0 = g * POOL_GROUP
        pm = _dot(pl_s[:, c0:c0 + POOL_GROUP], pmix[g]) * ps_ref[:, c0:c0 + POOL_GROUP]
        pm_s[:, c0:c0 + POOL_GROUP] = pm.astype(bf16)
    h = h_s[...]
    cn = cn_s[...]
    pm = pm_s[...]
    for c0 in range(0, D, COL_CHUNK):
        ga = _dot(h, w_in[:, 3 * D + c0:3 * D + c0 + COL_CHUNK])
        gb = _dot(h, w_in[:, 4 * D + c0:4 * D + c0 + COL_CHUNK])
        o_a = _dot(cn, w_co[:, c0:c0 + COL_CHUNK])
        o_b = _dot(pm, w_po[:, c0:c0 + COL_CHUNK])
        m_s[:, c0:c0 + COL_CHUNK] = (_sigmoid(ga) * o_a + _sigmoid(gb) * o_b).astype(bf16)
    m = m_s[...]
    for c0 in range(0, D, COL_CHUNK):
        g1 = _mod_cols(mod_ref, 2, c0, rows, rb)
        x1_s[:, c0:c0 + COL_CHUNK] = load_x_cols(c0) + g1 * _dot(m, w_out[:, c0:c0 + COL_CHUNK])
    _norm_mod(lambda r0: x1_s[r0:r0 + ROW_CHUNK, :], h2_s, mod_ref, 3, 4, rows, rb)


def _mlp(h2_s, hid_s, acc_s, w_f1, w_f2):
    h2 = h2_s[...]
    for f0 in range(0, D_FF, D):
        for c0 in range(0, D, COL_CHUNK):
            hid = jnp.maximum(_dot(h2, w_f1[:, f0 + c0:f0 + c0 + COL_CHUNK]), 0.0)
            hid_s[:, c0:c0 + COL_CHUNK] = (hid * hid).astype(bf16)
        hid = hid_s[...]
        for c0 in range(0, D, COL_CHUNK):
            part = _dot(hid, w_f2[f0:f0 + D, c0:c0 + COL_CHUNK])
            if f0 == 0:
                acc_s[:, c0:c0 + COL_CHUNK] = part
            else:
                acc_s[:, c0:c0 + COL_CHUNK] += part


def _final(rows, rb, mod_ref, x1_s, acc_s, fg_ref, store_y_rows):
    for r0 in range(0, rows, ROW_CHUNK):
        g2 = _mod_rows(mod_ref, 5, r0, rb)
        x2 = x1_s[r0:r0 + ROW_CHUNK, :] + g2 * acc_s[r0:r0 + ROW_CHUNK, :]
        store_y_rows(r0, _rms(x2) * fg_ref[...])


def _prompt_kernel(x_ref, mod_ref, w_in, w_co, pmix, w_po, w_out, w_f1, w_f2,
                   cw_ref, cb_ref, lg_ref, lb_ref, ps_ref, fg_ref,
                   y_ref, ncp_ref, npp_ref,
                   xrow, yrow, h_s, h2_s, a_ext, u_ext, cn_s, pl_s, pm_s, m_s, x1_s, acc_s, hid_s, cwb):
    j = pl.program_id(0)
    n_tiles = pl.num_programs(0) - 1
    a_h = CONV_HIST * BATCH
    u_h = POOL_HIST * BATCH

    @pl.when(j == 0)
    def _():
        a_ext[0:a_h, :] = jnp.zeros((a_h, D), f32)
        u_ext[0:u_h, :] = jnp.zeros((u_h, D), f32)
        h2_s[...] = jnp.zeros((TQ, D), bf16)
        x1_s[...] = jnp.zeros((TQ, D), f32)
        for k in range(CONV_W):
            cwb[k] = jnp.broadcast_to(cw_ref[k:k + 1, :], (SUBLANES, D))

    for b in range(BATCH):
        for t8 in range(0, TT, SUBLANES):
            for g in range(N_LANE_GROUPS):
                xrow[g, pl.ds(t8 * BATCH + b, SUBLANES, stride=BATCH), :] = (
                    x_ref[b, t8:t8 + SUBLANES, g * LANES:(g + 1) * LANES])

    def load_x_rows(r0):
        return jnp.concatenate([xrow[g, r0:r0 + ROW_CHUNK, :] for g in range(N_LANE_GROUPS)], axis=1)

    def load_x_cols(c0):
        g0 = c0 // LANES
        return jnp.concatenate([xrow[g0 + i] for i in range(COL_CHUNK // LANES)], axis=1)

    _norm_mod(load_x_rows, h_s, mod_ref, 0, 1, TQ, BATCH)
    _glu_and_pool_in(h_s, w_in, a_ext.at[a_h:a_h + TQ], u_ext.at[u_h:u_h + TQ])

    _mlp(h2_s, hid_s, acc_s, w_f1, w_f2)

    first = a_h - (CONV_W - 1) * BATCH
    reps = ROW_CHUNK // SUBLANES
    for r0 in range(0, TQ, ROW_CHUNK):
        parts = []
        for c0 in range(0, D, COL_CHUNK):
            acc = None
            for k in range(CONV_W):
                rk = r0 + first + k * BATCH
                wk = jnp.concatenate([cwb[k, :, c0:c0 + COL_CHUNK]] * reps, axis=0)
                term = a_ext[rk:rk + ROW_CHUNK, c0:c0 + COL_CHUNK] * wk
                acc = term if acc is None else acc + term
            parts.append(acc)
        cf = jnp.concatenate(parts, axis=1) + cb_ref[...]
        cn_s[r0:r0 + ROW_CHUNK, :] = _ln_silu(cf, lg_ref, lb_ref).astype(bf16)
        t_idx = j * TT + r0 // BATCH + lax.broadcasted_iota(jnp.int32, (ROW_CHUNK, 1), 0) // BATCH
        parts = []
        for gi, w in enumerate(POOL_WINDOWS):
            c0 = gi * POOL_GROUP
            acc = jnp.zeros((ROW_CHUNK, POOL_GROUP), f32)
            for jj in range(w):
                rk = r0 + u_h - jj * BATCH
                acc = acc + u_ext[rk:rk + ROW_CHUNK, c0:c0 + POOL_GROUP]
            inv_cnt = 1.0 / jnp.minimum(t_idx + 1, w).astype(f32)
            parts.append(acc * inv_cnt - u_ext[r0 + u_h:r0 + u_h + ROW_CHUNK, c0:c0 + POOL_GROUP])
        pl_s[r0:r0 + ROW_CHUNK, :] = jnp.concatenate(parts, axis=1).astype(bf16)

    def store_y_rows(r0, y):
        for g in range(N_LANE_GROUPS):
            yrow[g, r0:r0 + ROW_CHUNK, :] = y[:, g * LANES:(g + 1) * LANES]

    def deinterleave(dst_ref, n_t):
        for b in range(BATCH):
            for t8 in range(0, n_t, SUBLANES):
                for g in range(N_LANE_GROUPS):
                    dst_ref[b, t8:t8 + SUBLANES, g * LANES:(g + 1) * LANES] = (
                        yrow[g, pl.ds(t8 * BATCH + b, SUBLANES, stride=BATCH), :])

    _final(TQ, BATCH, mod_ref, x1_s, acc_s, fg_ref, store_y_rows)
    deinterleave(y_ref, TT)

    _merge(TQ, BATCH, load_x_cols, mod_ref, h_s, h2_s, cn_s, pl_s, pm_s, m_s, x1_s,
           w_in, w_co, pmix, w_po, w_out, ps_ref)

    a_ext[0:a_h, :] = a_ext[TQ:TQ + a_h, :]
    u_ext[0:u_h, :] = u_ext[TQ:TQ + u_h, :]

    @pl.when(j == n_tiles - 1)
    def _():
        for g in range(N_LANE_GROUPS):
            yrow[g, 0:a_h, :] = a_ext[0:a_h, g * LANES:(g + 1) * LANES]
        deinterleave(ncp_ref, CONV_HIST)
        for g in range(N_LANE_GROUPS):
            yrow[g, 0:u_h, :] = u_ext[0:u_h, g * LANES:(g + 1) * LANES]
        deinterleave(npp_ref, POOL_HIST)


def _weight_specs():
    return [
        _const_spec((D, 5 * D)),
        _const_spec((D, D)),
        _const_spec((len(POOL_WINDOWS), POOL_GROUP, POOL_GROUP)),
        _const_spec((D, D)),
        _const_spec((D, D)),
        _const_spec((D, D_FF)),
        _const_spec((D_FF, D)),
        _const_spec((CONV_W, D)),
        _const_spec((1, D)),
        _const_spec((1, D)),
        _const_spec((1, D)),
        _const_spec((1, D)),
        _const_spec((1, D)),
    ]


def _prompt_layer(x_prompt, mod, weights):
    n_tiles = SEQ // TT
    return pl.pallas_call(
        _prompt_kernel,
        grid=(n_tiles + 1,),
        in_specs=[
            pl.BlockSpec((BATCH, TT, D), lambda j: (0, jnp.minimum(j, n_tiles - 1), 0)),
            pl.BlockSpec((N_MOD, BATCH, D), lambda j: (0, DEC_BATCH // BATCH, 0)),
        ] + _weight_specs(),
        out_specs=[
            pl.BlockSpec((BATCH, TT, D), lambda j: (0, jnp.maximum(j - 1, 0), 0)),
            pl.BlockSpec((BATCH, CONV_HIST, D), lambda j: (0, 0, 0)),
            pl.BlockSpec((BATCH, POOL_HIST, D), lambda j: (0, 0, 0)),
        ],
        out_shape=[
            jax.ShapeDtypeStruct((BATCH, SEQ, D), f32),
            jax.ShapeDtypeStruct((BATCH, CONV_HIST, D), f32),
            jax.ShapeDtypeStruct((BATCH, POOL_HIST, D), f32),
        ],
        scratch_shapes=[
            pltpu.VMEM((N_LANE_GROUPS, TQ, LANES), f32),
            pltpu.VMEM((N_LANE_GROUPS, TQ, LANES), f32),
            pltpu.VMEM((TQ, D), bf16),
            pltpu.VMEM((TQ, D), bf16),
            pltpu.VMEM(((CONV_HIST + TT) * BATCH, D), f32),
            pltpu.VMEM(((POOL_HIST + TT) * BATCH, D), f32),
            pltpu.VMEM((TQ, D), bf16),
            pltpu.VMEM((TQ, D), bf16),
            pltpu.VMEM((TQ, D), bf16),
            pltpu.VMEM((TQ, D), bf16),
            pltpu.VMEM((TQ, D), f32),
            pltpu.VMEM((TQ, D), f32),
            pltpu.VMEM((TQ, D), bf16),
            pltpu.VMEM((CONV_W, SUBLANES, D), f32),
        ],
        compiler_params=pltpu.CompilerParams(
            dimension_semantics=("arbitrary",), vmem_limit_bytes=VMEM_LIMIT),
        name="prompt_layer",
    )(x_prompt, mod, *weights)


HIST_ROWS = 16


def _hist_kernel(sc_ref, sp_ref, cw_ref, hc_ref, hp_ref):
    for t in range(DEC_SEQ):
        for c0 in range(0, D, COL_CHUNK):
            acc = jnp.zeros((HIST_ROWS, COL_CHUNK), f32)
            for jh in range(t, CONV_W - 1):
                acc = acc + sc_ref[jh, :, c0:c0 + COL_CHUNK] * cw_ref[jh - t:jh - t + 1, c0:c0 + COL_CHUNK]
            hc_ref[t, :, c0:c0 + COL_CHUNK] = acc
        for gi, w in enumerate(POOL_WINDOWS):
            c0 = gi * POOL_GROUP
            acc = jnp.zeros((HIST_ROWS, POOL_GROUP), f32)
            for jj in range(t + 1, w):
                acc = acc + sp_ref[MAX_POOL - 1 + t - jj, :, c0:c0 + POOL_GROUP]
            hp_ref[t, :, c0:c0 + POOL_GROUP] = acc


def _sample_hist(sc_t, sp_t, conv_w):
    return pl.pallas_call(
        _hist_kernel,
        grid=(DEC_BATCH // HIST_ROWS,),
        in_specs=[
            pl.BlockSpec((CONV_W - 1, HIST_ROWS, D), lambda i: (0, i, 0)),
            pl.BlockSpec((MAX_POOL - 1, HIST_ROWS, D), lambda i: (0, i, 0)),
            pl.BlockSpec((CONV_W, D), lambda i: (0, 0)),
        ],
        out_specs=[
            pl.BlockSpec((DEC_SEQ, HIST_ROWS, D), lambda i: (0, i, 0)),
            pl.BlockSpec((DEC_SEQ, HIST_ROWS, D), lambda i: (0, i, 0)),
        ],
        out_shape=[
            jax.ShapeDtypeStruct((DEC_SEQ, DEC_BATCH, D), f32),
            jax.ShapeDtypeStruct((DEC_SEQ, DEC_BATCH, D), f32),
        ],
        name="sample_hist",
    )(sc_t, sp_t, conv_w)


def _sample_kernel(x_ref, hc_ref, hp_ref, mod_ref, w_in, w_co, pmix, w_po, w_out, w_f1, w_f2,
                   cw_ref, cb_ref, lg_ref, lb_ref, ps_ref, fg_ref,
                   y_ref, an_ref, un_ref,
                   a_all, u_all, h_s, h2_s, cn_s, pl_s, pm_s, m_s, x1_s, acc_s, hid_s):
    t = pl.program_id(0)
    rows = DEC_BATCH

    _norm_mod(lambda r0: x_ref[0, r0:r0 + ROW_CHUNK, :], h_s, mod_ref, 0, 1, rows, rows)
    _glu_and_pool_in(h_s, w_in, a_all.at[t], u_all.at[t])
    an_ref[0] = a_all[t]
    un_ref[0] = u_all[t]

    for r0 in range(0, rows, ROW_CHUNK):
        cf = hc_ref[0, r0:r0 + ROW_CHUNK, :] + cb_ref[...]
        for d in range(DEC_SEQ):
            prev = a_all[jnp.maximum(t - d, 0), r0:r0 + ROW_CHUNK, :]
            k = CONV_W - 1 - d
            cf = cf + jnp.where(t >= d, prev * cw_ref[k:k + 1, :], 0.0)
        cn_s[r0:r0 + ROW_CHUNK, :] = _ln_silu(cf, lg_ref, lb_ref).astype(bf16)
        parts = []
        for gi, w in enumerate(POOL_WINDOWS):
            c0 = gi * POOL_GROUP
            acc = hp_ref[0, r0:r0 + ROW_CHUNK, c0:c0 + POOL_GROUP]
            for d in range(min(w, DEC_SEQ)):
                prev = u_all[jnp.maximum(t - d, 0), r0:r0 + ROW_CHUNK, c0:c0 + POOL_GROUP]
                acc = acc + jnp.where(t >= d, prev, 0.0)
            cnt = jnp.minimum(jnp.full((ROW_CHUNK, 1), PAST_LEN + 1, jnp.int32) + t, w).astype(f32)
            parts.append(acc * (1.0 / cnt) - u_all[t, r0:r0 + ROW_CHUNK, c0:c0 + POOL_GROUP])
        pl_s[r0:r0 + ROW_CHUNK, :] = jnp.concatenate(parts, axis=1).astype(bf16)

    def load_x_cols(c0):
        return x_ref[0, :, c0:c0 + COL_CHUNK]

    def store_y_rows(r0, y):
        y_ref[0, r0:r0 + ROW_CHUNK, :] = y

    _merge(rows, rows, load_x_cols, mod_ref, h_s, h2_s, cn_s, pl_s, pm_s, m_s, x1_s,
           w_in, w_co, pmix, w_po, w_out, ps_ref)
    _mlp(h2_s, hid_s, acc_s, w_f1, w_f2)
    _final(rows, rows, mod_ref, x1_s, acc_s, fg_ref, store_y_rows)


def _sample_layer(xs_t, hc, hp, mod, weights):
    slab = lambda: pl.BlockSpec((1, DEC_BATCH, D), lambda t: (t, 0, 0))
    return pl.pallas_call(
        _sample_kernel,
        grid=(DEC_SEQ,),
        in_specs=[slab(), slab(), slab(),
                  pl.BlockSpec((N_MOD, DEC_BATCH, D), lambda t: (0, 0, 0))] + _weight_specs(),
        out_specs=[slab(), slab(), slab()],
        out_shape=[jax.ShapeDtypeStruct((DEC_SEQ, DEC_BATCH, D), f32)] * 3,
        scratch_shapes=[
            pltpu.VMEM((DEC_SEQ, DEC_BATCH, D), f32),
            pltpu.VMEM((DEC_SEQ, DEC_BATCH, D), f32),
            pltpu.VMEM((DEC_BATCH, D), bf16),
            pltpu.VMEM((DEC_BATCH, D), bf16),
            pltpu.VMEM((DEC_BATCH, D), bf16),
            pltpu.VMEM((DEC_BATCH, D), bf16),
            pltpu.VMEM((DEC_BATCH, D), bf16),
            pltpu.VMEM((DEC_BATCH, D), bf16),
            pltpu.VMEM((DEC_BATCH, D), f32),
            pltpu.VMEM((DEC_BATCH, D), f32),
            pltpu.VMEM((DEC_BATCH, D), bf16),
        ],
        compiler_params=pltpu.CompilerParams(
            dimension_semantics=("arbitrary",), vmem_limit_bytes=VMEM_LIMIT),
        name="sample_layer",
    )(xs_t, hc, hp, mod, *weights)


STATE_ROWS = 8


def _state_kernel(sc_ref, sp_ref, an_ref, un_ref, nc_ref, np_ref):
    for old, new, out, n_hist in ((sc_ref, an_ref, nc_ref, CONV_W - 1), (sp_ref, un_ref, np_ref, MAX_POOL - 1)):
        keep = n_hist - DEC_SEQ
        for b in range(STATE_ROWS):
            out[b, 0:keep, :] = old[b, DEC_SEQ:n_hist, :]
            for t in range(DEC_SEQ):
                out[b, keep + t:keep + t + 1, :] = new[t, b:b + 1, :]


def _sample_state(sc, sp, a_new, u_new):
    return pl.pallas_call(
        _state_kernel,
        grid=(DEC_BATCH // STATE_ROWS,),
        in_specs=[
            pl.BlockSpec((STATE_ROWS, CONV_W - 1, D), lambda i: (i, 0, 0)),
            pl.BlockSpec((STATE_ROWS, MAX_POOL - 1, D), lambda i: (i, 0, 0)),
            pl.BlockSpec((DEC_SEQ, STATE_ROWS, D), lambda i: (0, i, 0)),
            pl.BlockSpec((DEC_SEQ, STATE_ROWS, D), lambda i: (0, i, 0)),
        ],
        out_specs=[
            pl.BlockSpec((STATE_ROWS, CONV_W - 1, D), lambda i: (i, 0, 0)),
            pl.BlockSpec((STATE_ROWS, MAX_POOL - 1, D), lambda i: (i, 0, 0)),
        ],
        out_shape=[
            jax.ShapeDtypeStruct((DEC_BATCH, CONV_W - 1, D), f32),
            jax.ShapeDtypeStruct((DEC_BATCH, MAX_POOL - 1, D), f32),
        ],
        name="sample_state",
    )(sc, sp, a_new, u_new)


def kernel(x_prompt, x_sample, state_conv, state_pool, c_prompt, c_sample, w_ada, b_ada, w_in,
           conv_w, conv_b, ln_g, ln_b, w_conv_out, pool_mix, pool_scale, w_pool_out, w_out,
           w_ff1, w_ff2, final_g):
    assert w_in.shape[0] == 1, "single layer"
    mod = _ada(c_sample, c_prompt, w_ada[0], b_ada[0])
    weights = (
        w_in[0].astype(bf16), w_conv_out[0].astype(bf16), pool_mix[0].astype(bf16),
        w_pool_out[0].astype(bf16), w_out[0].astype(bf16), w_ff1[0].astype(bf16), w_ff2[0].astype(bf16),
        conv_w[0], conv_b, ln_g, ln_b, pool_scale, final_g.reshape(1, D),
    )
    y_prompt, ncp, npp = _prompt_layer(x_prompt, mod, weights)
    new_conv_prompt = ncp[None, :, CONV_HIST - (CONV_W - 1):, :]
    new_pool_prompt = npp[None, :, POOL_HIST - (MAX_POOL - 1):, :]

    sc = state_conv[0]
    sp = state_pool[0]
    hc, hp = _sample_hist(sc.transpose(1, 0, 2), sp.transpose(1, 0, 2), conv_w[0])
    ys_t, a_new, u_new = _sample_layer(x_sample.transpose(1, 0, 2), hc, hp, mod, weights)
    y_sample = ys_t.transpose(1, 0, 2)
    ncs, nps = _sample_state(sc, sp, a_new, u_new)
    return (y_prompt, y_sample, new_conv_prompt, new_pool_prompt, ncs[None], nps[None])
```

```python
import jax
import jax.numpy as jnp
from jax import lax
from jax.experimental import pallas as pl
from jax.experimental.pallas import tpu as pltpu

D = 1024
BATCH = 8
SEQ = 2048
DEC_BATCH = 128
DEC_SEQ = 4
PAST_LEN = 16384
CONV_W = 31
POOL_WINDOWS = (2, 4, 8, 16)
POOL_GROUP = D // len(POOL_WINDOWS)
MAX_POOL = max(POOL_WINDOWS)
D_FF = 4 * D
N_MOD = 6
EPS = 1e-6

SUBLANES = 8
LANES = 128
N_LANE_GROUPS = D // LANES
ROW_CHUNK = 32
COL_CHUNK = 256
TT = 32
TQ = TT * BATCH
CONV_HIST = 32
POOL_HIST = 16
VMEM_LIMIT = 58 * 1024 * 1024

f32 = jnp.float32
bf16 = jnp.bfloat16


def _dot(a, b):
    return jnp.dot(a, b, preferred_element_type=f32)


def _sigmoid(x):
    return 1.0 / (1.0 + jnp.exp(-x))


def _rms(x):
    return x * lax.rsqrt(jnp.mean(x * x, axis=-1, keepdims=True) + EPS)


def _const_spec(shape):
    nd = len(shape)
    return pl.BlockSpec(shape, lambda *_: (0,) * nd, pipeline_mode=pl.Buffered(1))


def _ada_kernel(cs_ref, cp_ref, w_ref, b_ref, o_ref):
    w = w_ref[...].astype(bf16)
    b = b_ref[0]
    cs = cs_ref[...]
    cp = cp_ref[...]
    o_ref[0, 0:DEC_BATCH, :] = _dot((cs * _sigmoid(cs)).astype(bf16), w) + b
    o_ref[0, DEC_BATCH:DEC_BATCH + BATCH, :] = _dot((cp * _sigmoid(cp)).astype(bf16), w) + b


def _ada(c_sample, c_prompt, w_ada, b_ada):
    return pl.pallas_call(
        _ada_kernel,
        grid=(N_MOD,),
        in_specs=[
            pl.BlockSpec((DEC_BATCH, D), lambda i: (0, 0)),
            pl.BlockSpec((BATCH, D), lambda i: (0, 0)),
            pl.BlockSpec((D, D), lambda i: (0, i)),
            pl.BlockSpec((1, 1, D), lambda i: (i, 0, 0)),
        ],
        out_specs=pl.BlockSpec((1, DEC_BATCH + BATCH, D), lambda i: (i, 0, 0)),
        out_shape=jax.ShapeDtypeStruct((N_MOD, DEC_BATCH + BATCH, D), f32),
        name="ada_mod",
    )(c_sample, c_prompt, w_ada, b_ada.reshape(N_MOD, 1, D))


def _mod_rows(mod_ref, i, r0, rb):
    if rb < ROW_CHUNK:
        m = mod_ref[i]
        return jnp.concatenate([m] * (ROW_CHUNK // rb), axis=0)
    return mod_ref[i, r0:r0 + ROW_CHUNK, :]


def _mod_cols(mod_ref, i, c0, rows, rb):
    m = mod_ref[i, :, c0:c0 + COL_CHUNK]
    if rb < rows:
        return jnp.concatenate([m] * (rows // rb), axis=0)
    return m


def _glu_and_pool_in(h_s, w_in, a_dst, u_dst):
    h = h_s[...]
    for c0 in range(0, D, COL_CHUNK):
        val = _dot(h, w_in[:, c0:c0 + COL_CHUNK])
        gate = _dot(h, w_in[:, D + c0:D + c0 + COL_CHUNK])
        a_dst[:, c0:c0 + COL_CHUNK] = val * _sigmoid(gate)
        u_dst[:, c0:c0 + COL_CHUNK] = _dot(h, w_in[:, 2 * D + c0:2 * D + c0 + COL_CHUNK])


def _ln_silu(cf, lg_ref, lb_ref):
    mu = jnp.mean(cf, axis=-1, keepdims=True)
    cc = cf - mu
    var = jnp.mean(cc * cc, axis=-1, keepdims=True)
    cn = cc * lax.rsqrt(var + EPS) * lg_ref[...] + lb_ref[...]
    return cn * _sigmoid(cn)


def _norm_mod(load_rows, dst, mod_ref, i_shift, i_scale, rows, rb):
    for r0 in range(0, rows, ROW_CHUNK):
        x = load_rows(r0)
        shift = _mod_rows(mod_ref, i_shift, r0, rb)
        scale = _mod_rows(mod_ref, i_scale, r0, rb)
        dst[r0:r0 + ROW_CHUNK, :] = (_rms(x) * (1.0 + scale) + shift).astype(bf16)


def _merge(rows, rb, load_x_cols, mod_ref, h_s, h2_s, cn_s, pl_s, pm_s, m_s, x1_s,
           w_in, w_co, pmix, w_po, w_out, ps_ref):
    for g in range(len(POOL_WINDOWS)):
        c0 = g * POOL_GROUP
        pm = _dot(pl_s[:, c0:c0 + POOL_GROUP], pmix[g]) * ps_ref[:, c0:c0 + POOL_GROUP]
        pm_s[:, c0:c0 + POOL_GROUP] = pm.astype(bf16)
    h = h_s[...]
    cn = cn_s[...]
    pm = pm_s[...]
    for c0 in range(0, D, COL_CHUNK):
        ga = _dot(h, w_in[:, 3 * D + c0:3 * D + c0 + COL_CHUNK])
        gb = _dot(h, w_in[:, 4 * D + c0:4 * D + c0 + COL_CHUNK])
        o_a = _dot(cn, w_co[:, c0:c0 + COL_CHUNK])
        o_b = _dot(pm, w_po[:, c0:c0 + COL_CHUNK])
        m_s[:, c0:c0 + COL_CHUNK] = (_sigmoid(ga) * o_a + _sigmoid(gb) * o_b).astype(bf16)
    m = m_s[...]
    for c0 in range(0, D, COL_CHUNK):
        g1 = _mod_cols(mod_ref, 2, c0, rows, rb)
        x1_s[:, c0:c0 + COL_CHUNK] = load_x_cols(c0) + g1 * _dot(m, w_out[:, c0:c0 + COL_CHUNK])
    _norm_mod(lambda r0: x1_s[r0:r0 + ROW_CHUNK, :], h2_s, mod_ref, 3, 4, rows, rb)


def _mlp(h2_s, hid_s, acc_s, w_f1, w_f2):
    h2 = h2_s[...]
    for f0 in range(0, D_FF, D):
        for c0 in range(0, D, COL_CHUNK):
            hid = jnp.maximum(_dot(h2, w_f1[:, f0 + c0:f0 + c0 + COL_CHUNK]), 0.0)
            hid_s[:, c0:c0 + COL_CHUNK] = (hid * hid).astype(bf16)
        hid = hid_s[...]
        for c0 in range(0, D, COL_CHUNK):
            part = _dot(hid, w_f2[f0:f0 + D, c0:c0 + COL_CHUNK])
            if f0 == 0:
                acc_s[:, c0:c0 + COL_CHUNK] = part
            else:
                acc_s[:, c0:c0 + COL_CHUNK] += part


def _final(rows, rb, mod_ref, x1_s, acc_s, fg_ref, store_y_rows):
    for r0 in range(0, rows, ROW_CHUNK):
        g2 = _mod_rows(mod_ref, 5, r0, rb)
        x2 = x1_s[r0:r0 + ROW_CHUNK, :] + g2 * acc_s[r0:r0 + ROW_CHUNK, :]
        store_y_rows(r0, _rms(x2) * fg_ref[...])


def _prompt_kernel(x_ref, mod_ref, w_in, w_co, pmix, w_po, w_out, w_f1, w_f2,
                   cw_ref, cb_ref, lg_ref, lb_ref, ps_ref, fg_ref,
                   y_ref, ncp_ref, npp_ref,
                   xrow, yrow, h_s, h2_s, a_ext, u_ext, cn_s, pl_s, pm_s, m_s, x1_s, acc_s, hid_s):
    j = pl.program_id(0)
    n_tiles = pl.num_programs(0) - 1
    a_h = CONV_HIST * BATCH
    u_h = POOL_HIST * BATCH

    @pl.when(j == 0)
    def _():
        a_ext[0:a_h, :] = jnp.zeros((a_h, D), f32)
        u_ext[0:u_h, :] = jnp.zeros((u_h, D), f32)
        h2_s[...] = jnp.zeros((TQ, D), bf16)
        x1_s[...] = jnp.zeros((TQ, D), f32)

    def store_y_rows(r0, y):
        for g in range(N_LANE_GROUPS):
            yrow[g, r0:r0 + ROW_CHUNK, :] = y[:, g * LANES:(g + 1) * LANES]

    def deinterleave(dst_ref, n_t):
        for b in range(BATCH):
            for t8 in range(0, n_t, SUBLANES):
                for g in range(N_LANE_GROUPS):
                    dst_ref[b, t8:t8 + SUBLANES, g * LANES:(g + 1) * LANES] = (
                        yrow[g, pl.ds(t8 * BATCH + b, SUBLANES, stride=BATCH), :])

    @pl.when(j < n_tiles)
    def _():
        for b in range(BATCH):
            for t8 in range(0, TT, SUBLANES):
                for g in range(N_LANE_GROUPS):
                    xrow[g, pl.ds(t8 * BATCH + b, SUBLANES, stride=BATCH), :] = (
                        x_ref[b, t8:t8 + SUBLANES, g * LANES:(g + 1) * LANES])

        def load_x_rows(r0):
            return jnp.concatenate([xrow[g, r0:r0 + ROW_CHUNK, :] for g in range(N_LANE_GROUPS)], axis=1)

        def load_x_cols(c0):
            g0 = c0 // LANES
            return jnp.concatenate([xrow[g0 + i] for i in range(COL_CHUNK // LANES)], axis=1)

        _norm_mod(load_x_rows, h_s, mod_ref, 0, 1, TQ, BATCH)
        _glu_and_pool_in(h_s, w_in, a_ext.at[a_h:a_h + TQ], u_ext.at[u_h:u_h + TQ])

        _mlp(h2_s, hid_s, acc_s, w_f1, w_f2)

        first = a_h - (CONV_W - 1) * BATCH
        for r0 in range(0, TQ, ROW_CHUNK):
            parts = []
            for c0 in range(0, D, COL_CHUNK):
                acc = jnp.zeros((ROW_CHUNK, COL_CHUNK), f32)
                for k in range(CONV_W):
                    rk = r0 + first + k * BATCH
                    acc = acc + a_ext[rk:rk + ROW_CHUNK, c0:c0 + COL_CHUNK] * cw_ref[k:k + 1, c0:c0 + COL_CHUNK]
                parts.append(acc)
            cf = jnp.concatenate(parts, axis=1) + cb_ref[...]
            cn_s[r0:r0 + ROW_CHUNK, :] = _ln_silu(cf, lg_ref, lb_ref).astype(bf16)
            t_idx = j * TT + r0 // BATCH + lax.broadcasted_iota(jnp.int32, (ROW_CHUNK, 1), 0) // BATCH
            parts = []
            for gi, w in enumerate(POOL_WINDOWS):
                c0 = gi * POOL_GROUP
                acc = jnp.zeros((ROW_CHUNK, POOL_GROUP), f32)
                for jj in range(w):
                    rk = r0 + u_h - jj * BATCH
                    acc = acc + u_ext[rk:rk + ROW_CHUNK, c0:c0 + POOL_GROUP]
                inv_cnt = 1.0 / jnp.minimum(t_idx + 1, w).astype(f32)
                parts.append(acc * inv_cnt - u_ext[r0 + u_h:r0 + u_h + ROW_CHUNK, c0:c0 + POOL_GROUP])
            pl_s[r0:r0 + ROW_CHUNK, :] = jnp.concatenate(parts, axis=1).astype(bf16)

        _final(TQ, BATCH, mod_ref, x1_s, acc_s, fg_ref, store_y_rows)
        deinterleave(y_ref, TT)

        _merge(TQ, BATCH, load_x_cols, mod_ref, h_s, h2_s, cn_s, pl_s, pm_s, m_s, x1_s,
               w_in, w_co, pmix, w_po, w_out, ps_ref)

        a_ext[0:a_h, :] = a_ext[TQ:TQ + a_h, :]
        u_ext[0:u_h, :] = u_ext[TQ:TQ + u_h, :]

    @pl.when(j == n_tiles)
    def _():
        _mlp(h2_s, hid_s, acc_s, w_f1, w_f2)
        _final(TQ, BATCH, mod_ref, x1_s, acc_s, fg_ref, store_y_rows)
        deinterleave(y_ref, TT)

    @pl.when(j == n_tiles - 1)
    def _():
        for g in range(N_LANE_GROUPS):
            yrow[g, 0:a_h, :] = a_ext[0:a_h, g * LANES:(g + 1) * LANES]
        deinterleave(ncp_ref, CONV_HIST)
        for g in range(N_LANE_GROUPS):
            yrow[g, 0:u_h, :] = u_ext[0:u_h, g * LANES:(g + 1) * LANES]
        deinterleave(npp_ref, POOL_HIST)


def _weight_specs():
    return [
        _const_spec((D, 5 * D)),
        _const_spec((D, D)),
        _const_spec((len(POOL_WINDOWS), POOL_GROUP, POOL_GROUP)),
        _const_spec((D, D)),
        _const_spec((D, D)),
        _const_spec((D, D_FF)),
        _const_spec((D_FF, D)),
        _const_spec((CONV_W, D)),
        _const_spec((1, D)),
        _const_spec((1, D)),
        _const_spec((1, D)),
        _const_spec((1, D)),
        _const_spec((1, D)),
    ]


def _prompt_layer(x_prompt, mod, weights):
    n_tiles = SEQ // TT
    return pl.pallas_call(
        _prompt_kernel,
        grid=(n_tiles + 1,),
        in_specs=[
            pl.BlockSpec((BATCH, TT, D), lambda j: (0, jnp.minimum(j, n_tiles - 1), 0)),
            pl.BlockSpec((N_MOD, BATCH, D), lambda j: (0, DEC_BATCH // BATCH, 0)),
        ] + _weight_specs(),
        out_specs=[
            pl.BlockSpec((BATCH, TT, D), lambda j: (0, jnp.maximum(j - 1, 0), 0)),
            pl.BlockSpec((BATCH, CONV_HIST, D), lambda j: (0, 0, 0)),
            pl.BlockSpec((BATCH, POOL_HIST, D), lambda j: (0, 0, 0)),
        ],
        out_shape=[
            jax.ShapeDtypeStruct((BATCH, SEQ, D), f32),
            jax.ShapeDtypeStruct((BATCH, CONV_HIST, D), f32),
            jax.ShapeDtypeStruct((BATCH, POOL_HIST, D), f32),
        ],
        scratch_shapes=[
            pltpu.VMEM((N_LANE_GROUPS, TQ, LANES), f32),
            pltpu.VMEM((N_LANE_GROUPS, TQ, LANES), f32),
            pltpu.VMEM((TQ, D), bf16),
            pltpu.VMEM((TQ, D), bf16),
            pltpu.VMEM(((CONV_HIST + TT) * BATCH, D), f32),
            pltpu.VMEM(((POOL_HIST + TT) * BATCH, D), f32),
            pltpu.VMEM((TQ, D), bf16),
            pltpu.VMEM((TQ, D), bf16),
            pltpu.VMEM((TQ, D), bf16),
            pltpu.VMEM((TQ, D), bf16),
            pltpu.VMEM((TQ, D), f32),
            pltpu.VMEM((TQ, D), f32),
            pltpu.VMEM((TQ, D), bf16),
        ],
        compiler_params=pltpu.CompilerParams(
            dimension_semantics=("arbitrary",), vmem_limit_bytes=VMEM_LIMIT),
        name="prompt_layer",
    )(x_prompt, mod, *weights)


HIST_ROWS = 16


def _hist_kernel(sc_ref, sp_ref, cw_ref, hc_ref, hp_ref):
    for t in range(DEC_SEQ):
        for c0 in range(0, D, COL_CHUNK):
            acc = jnp.zeros((HIST_ROWS, COL_CHUNK), f32)
            for jh in range(t, CONV_W - 1):
                acc = acc + sc_ref[jh, :, c0:c0 + COL_CHUNK] * cw_ref[jh - t:jh - t + 1, c0:c0 + COL_CHUNK]
            hc_ref[t, :, c0:c0 + COL_CHUNK] = acc
        for gi, w in enumerate(POOL_WINDOWS):
            c0 = gi * POOL_GROUP
            acc = jnp.zeros((HIST_ROWS, POOL_GROUP), f32)
            for jj in range(t + 1, w):
                acc = acc + sp_ref[MAX_POOL - 1 + t - jj, :, c0:c0 + POOL_GROUP]
            hp_ref[t, :, c0:c0 + POOL_GROUP] = acc


def _sample_hist(sc_t, sp_t, conv_w):
    return pl.pallas_call(
        _hist_kernel,
        grid=(DEC_BATCH // HIST_ROWS,),
        in_specs=[
            pl.BlockSpec((CONV_W - 1, HIST_ROWS, D), lambda i: (0, i, 0)),
            pl.BlockSpec((MAX_POOL - 1, HIST_ROWS, D), lambda i: (0, i, 0)),
            pl.BlockSpec((CONV_W, D), lambda i: (0, 0)),
        ],
        out_specs=[
            pl.BlockSpec((DEC_SEQ, HIST_ROWS, D), lambda i: (0, i, 0)),
            pl.BlockSpec((DEC_SEQ, HIST_ROWS, D), lambda i: (0, i, 0)),
        ],
        out_shape=[
            jax.ShapeDtypeStruct((DEC_SEQ, DEC_BATCH, D), f32),
            jax.ShapeDtypeStruct((DEC_SEQ, DEC_BATCH, D), f32),
        ],
        name="sample_hist",
    )(sc_t, sp_t, conv_w)


def _sample_kernel(x_ref, hc_ref, hp_ref, mod_ref, w_in, w_co, pmix, w_po, w_out, w_f1, w_f2,
                   cw_ref, cb_ref, lg_ref, lb_ref, ps_ref, fg_ref,
                   y_ref, an_ref, un_ref,
                   a_all, u_all, h_s, h2_s, cn_s, pl_s, pm_s, m_s, x1_s, acc_s, hid_s):
    t = pl.program_id(0)
    rows = DEC_BATCH

    _norm_mod(lambda r0: x_ref[0, r0:r0 + ROW_CHUNK, :], h_s, mod_ref, 0, 1, rows, rows)
    _glu_and_pool_in(h_s, w_in, a_all.at[t], u_all.at[t])
    an_ref[0] = a_all[t]
    un_ref[0] = u_all[t]

    for r0 in range(0, rows, ROW_CHUNK):
        cf = hc_ref[0, r0:r0 + ROW_CHUNK, :] + cb_ref[...]
        for d in range(DEC_SEQ):
            prev = a_all[jnp.maximum(t - d, 0), r0:r0 + ROW_CHUNK, :]
            k = CONV_W - 1 - d
            cf = cf + jnp.where(t >= d, prev * cw_ref[k:k + 1, :], 0.0)
        cn_s[r0:r0 + ROW_CHUNK, :] = _ln_silu(cf, lg_ref, lb_ref).astype(bf16)
        parts = []
        for gi, w in enumerate(POOL_WINDOWS):
            c0 = gi * POOL_GROUP
            acc = hp_ref[0, r0:r0 + ROW_CHUNK, c0:c0 + POOL_GROUP]
            for d in range(min(w, DEC_SEQ)):
                prev = u_all[jnp.maximum(t - d, 0), r0:r0 + ROW_CHUNK, c0:c0 + POOL_GROUP]
                acc = acc + jnp.where(t >= d, prev, 0.0)
            cnt = jnp.minimum(jnp.full((ROW_CHUNK, 1), PAST_LEN + 1, jnp.int32) + t, w).astype(f32)
            parts.append(acc * (1.0 / cnt) - u_all[t, r0:r0 + ROW_CHUNK, c0:c0 + POOL_GROUP])
        pl_s[r0:r0 + ROW_CHUNK, :] = jnp.concatenate(parts, axis=1).astype(bf16)

    def load_x_cols(c0):
        return x_ref[0, :, c0:c0 + COL_CHUNK]

    def store_y_rows(r0, y):
        y_ref[0, r0:r0 + ROW_CHUNK, :] = y

    _merge(rows, rows, load_x_cols, mod_ref, h_s, h2_s, cn_s, pl_s, pm_s, m_s, x1_s,
           w_in, w_co, pmix, w_po, w_out, ps_ref)
    _mlp(h2_s, hid_s, acc_s, w_f1, w_f2)
    _final(rows, rows, mod_ref, x1_s, acc_s, fg_ref, store_y_rows)


def _sample_layer(xs_t, hc, hp, mod, weights):
    slab = lambda: pl.BlockSpec((1, DEC_BATCH, D), lambda t: (t, 0, 0))
    return pl.pallas_call(
        _sample_kernel,
        grid=(DEC_SEQ,),
        in_specs=[slab(), slab(), slab(),
                  pl.BlockSpec((N_MOD, DEC_BATCH, D), lambda t: (0, 0, 0))] + _weight_specs(),
        out_specs=[slab(), slab(), slab()],
        out_shape=[jax.ShapeDtypeStruct((DEC_SEQ, DEC_BATCH, D), f32)] * 3,
        scratch_shapes=[
            pltpu.VMEM((DEC_SEQ, DEC_BATCH, D), f32),
            pltpu.VMEM((DEC_SEQ, DEC_BATCH, D), f32),
            pltpu.VMEM((DEC_BATCH, D), bf16),
            pltpu.VMEM((DEC_BATCH, D), bf16),
            pltpu.VMEM((DEC_BATCH, D), bf16),
            pltpu.VMEM((DEC_BATCH, D), bf16),
            pltpu.VMEM((DEC_BATCH, D), bf16),
            pltpu.VMEM((DEC_BATCH, D), bf16),
            pltpu.VMEM((DEC_BATCH, D), f32),
            pltpu.VMEM((DEC_BATCH, D), f32),
            pltpu.VMEM((DEC_BATCH, D), bf16),
        ],
        compiler_params=pltpu.CompilerParams(
            dimension_semantics=("arbitrary",), vmem_limit_bytes=VMEM_LIMIT),
        name="sample_layer",
    )(xs_t, hc, hp, mod, *weights)


def kernel(x_prompt, x_sample, state_conv, state_pool, c_prompt, c_sample, w_ada, b_ada, w_in,
           conv_w, conv_b, ln_g, ln_b, w_conv_out, pool_mix, pool_scale, w_pool_out, w_out,
           w_ff1, w_ff2, final_g):
    assert w_in.shape[0] == 1, "single layer"
    mod = _ada(c_sample, c_prompt, w_ada[0], b_ada[0])
    weights = (
        w_in[0].astype(bf16), w_conv_out[0].astype(bf16), pool_mix[0].astype(bf16),
        w_pool_out[0].astype(bf16), w_out[0].astype(bf16), w_ff1[0].astype(bf16), w_ff2[0].astype(bf16),
        conv_w[0], conv_b, ln_g, ln_b, pool_scale, final_g.reshape(1, D),
    )
    y_prompt, ncp, npp = _prompt_layer(x_prompt, mod, weights)
    new_conv_prompt = ncp[None, :, CONV_HIST - (CONV_W - 1):, :]
    new_pool_prompt = npp[None, :, POOL_HIST - (MAX_POOL - 1):, :]

    sc = state_conv[0]
    sp = state_pool[0]
    hc, hp = _sample_hist(sc.transpose(1, 0, 2), sp.transpose(1, 0, 2), conv_w[0])
    ys_t, a_new, u_new = _sample_layer(x_sample.transpose(1, 0, 2), hc, hp, mod, weights)
    y_sample = ys_t.transpose(1, 0, 2)
    new_conv_sample = jnp.concatenate([sc[:, DEC_SEQ:], a_new.transpose(1, 0, 2)], axis=1)[None]
    new_pool_sample = jnp.concatenate([sp[:, DEC_SEQ:], u_new.transpose(1, 0, 2)], axis=1)[None]
    return (y_prompt, y_sample, new_conv_prompt, new_pool_prompt, new_conv_sample, new_pool_sample)
```

```python
import jax
import jax.numpy as jnp
from jax import lax
from jax.experimental import pallas as pl
from jax.experimental.pallas import tpu as pltpu

D = 1024
BATCH = 8
SEQ = 2048
DEC_BATCH = 128
DEC_SEQ = 4
PAST_LEN = 16384
CONV_W = 31
POOL_WINDOWS = (2, 4, 8, 16)
POOL_GROUP = D // len(POOL_WINDOWS)
MAX_POOL = max(POOL_WINDOWS)
D_FF = 4 * D
N_MOD = 6
EPS = 1e-6

SUBLANES = 8
LANES = 128
N_LANE_GROUPS = D // LANES
ROW_CHUNK = 32
COL_CHUNK = 256
TT = 32
TQ = TT * BATCH
CONV_HIST = 32
POOL_HIST = 16
VMEM_LIMIT = 58 * 1024 * 1024

f32 = jnp.float32
bf16 = jnp.bfloat16


def _dot(a, b):
    return jnp.dot(a, b, preferred_element_type=f32)


def _sigmoid(x):
    return 1.0 / (1.0 + jnp.exp(-x))


def _rms(x):
    return x * lax.rsqrt(jnp.mean(x * x, axis=-1, keepdims=True) + EPS)


def _const_spec(shape):
    nd = len(shape)
    return pl.BlockSpec(shape, lambda *_: (0,) * nd, pipeline_mode=pl.Buffered(1))


def _ada_kernel(cs_ref, cp_ref, w_ref, b_ref, o_ref):
    w = w_ref[...].astype(bf16)
    b = b_ref[0]
    cs = cs_ref[...]
    cp = cp_ref[...]
    o_ref[0, 0:DEC_BATCH, :] = _dot((cs * _sigmoid(cs)).astype(bf16), w) + b
    o_ref[0, DEC_BATCH:DEC_BATCH + BATCH, :] = _dot((cp * _sigmoid(cp)).astype(bf16), w) + b


def _ada(c_sample, c_prompt, w_ada, b_ada):
    return pl.pallas_call(
        _ada_kernel,
        grid=(N_MOD,),
        in_specs=[
            pl.BlockSpec((DEC_BATCH, D), lambda i: (0, 0)),
            pl.BlockSpec((BATCH, D), lambda i: (0, 0)),
            pl.BlockSpec((D, D), lambda i: (0, i)),
            pl.BlockSpec((1, 1, D), lambda i: (i, 0, 0)),
        ],
        out_specs=pl.BlockSpec((1, DEC_BATCH + BATCH, D), lambda i: (i, 0, 0)),
        out_shape=jax.ShapeDtypeStruct((N_MOD, DEC_BATCH + BATCH, D), f32),
        name="ada_mod",
    )(c_sample, c_prompt, w_ada, b_ada.reshape(N_MOD, 1, D))


def _mod_rows(mod_ref, i, r0, rb):
    if rb < ROW_CHUNK:
        m = mod_ref[i]
        return jnp.concatenate([m] * (ROW_CHUNK // rb), axis=0)
    return mod_ref[i, r0:r0 + ROW_CHUNK, :]


def _mod_cols(mod_ref, i, c0, rows, rb):
    m = mod_ref[i, :, c0:c0 + COL_CHUNK]
    if rb < rows:
        return jnp.concatenate([m] * (rows // rb), axis=0)
    return m


def _glu_and_pool_in(h_s, w_in, a_dst, u_dst):
    h = h_s[...]
    for c0 in range(0, D, COL_CHUNK):
        val = _dot(h, w_in[:, c0:c0 + COL_CHUNK])
        gate = _dot(h, w_in[:, D + c0:D + c0 + COL_CHUNK])
        a_dst[:, c0:c0 + COL_CHUNK] = val * _sigmoid(gate)
        u_dst[:, c0:c0 + COL_CHUNK] = _dot(h, w_in[:, 2 * D + c0:2 * D + c0 + COL_CHUNK])


def _ln_silu(cf, lg_ref, lb_ref):
    mu = jnp.mean(cf, axis=-1, keepdims=True)
    cc = cf - mu
    var = jnp.mean(cc * cc, axis=-1, keepdims=True)
    cn = cc * lax.rsqrt(var + EPS) * lg_ref[...] + lb_ref[...]
    return cn * _sigmoid(cn)


def _norm_mod(load_rows, dst, mod_ref, i_shift, i_scale, rows, rb):
    for r0 in range(0, rows, ROW_CHUNK):
        x = load_rows(r0)
        shift = _mod_rows(mod_ref, i_shift, r0, rb)
        scale = _mod_rows(mod_ref, i_scale, r0, rb)
        dst[r0:r0 + ROW_CHUNK, :] = (_rms(x) * (1.0 + scale) + shift).astype(bf16)


def _merge(rows, rb, load_x_cols, mod_ref, h_s, h2_s, cn_s, pl_s, pm_s, m_s, x1_s,
           w_in, w_co, pmix, w_po, w_out, ps_ref):
    for g in range(len(POOL_WINDOWS)):
        c0 = g * POOL_GROUP
        pm = _dot(pl_s[:, c0:c0 + POOL_GROUP], pmix[g]) * ps_ref[:, c0:c0 + POOL_GROUP]
        pm_s[:, c0:c0 + POOL_GROUP] = pm.astype(bf16)
    h = h_s[...]
    cn = cn_s[...]
    pm = pm_s[...]
    for c0 in range(0, D, COL_CHUNK):
        ga = _dot(h, w_in[:, 3 * D + c0:3 * D + c0 + COL_CHUNK])
        gb = _dot(h, w_in[:, 4 * D + c0:4 * D + c0 + COL_CHUNK])
        o_a = _dot(cn, w_co[:, c0:c0 + COL_CHUNK])
        o_b = _dot(pm, w_po[:, c0:c0 + COL_CHUNK])
        m_s[:, c0:c0 + COL_CHUNK] = (_sigmoid(ga) * o_a + _sigmoid(gb) * o_b).astype(bf16)
    m = m_s[...]
    for c0 in range(0, D, COL_CHUNK):
        g1 = _mod_cols(mod_ref, 2, c0, rows, rb)
        x1_s[:, c0:c0 + COL_CHUNK] = load_x_cols(c0) + g1 * _dot(m, w_out[:, c0:c0 + COL_CHUNK])
    _norm_mod(lambda r0: x1_s[r0:r0 + ROW_CHUNK, :], h2_s, mod_ref, 3, 4, rows, rb)


def _mlp(h2_s, hid_s, acc_s, w_f1, w_f2):
    h2 = h2_s[...]
    for f0 in range(0, D_FF, D):
        for c0 in range(0, D, COL_CHUNK):
            hid = jnp.maximum(_dot(h2, w_f1[:, f0 + c0:f0 + c0 + COL_CHUNK]), 0.0)
            hid_s[:, c0:c0 + COL_CHUNK] = (hid * hid).astype(bf16)
        hid = hid_s[...]
        for c0 in range(0, D, COL_CHUNK):
            part = _dot(hid, w_f2[f0:f0 + D, c0:c0 + COL_CHUNK])
            if f0 == 0:
                acc_s[:, c0:c0 + COL_CHUNK] = part
            else:
                acc_s[:, c0:c0 + COL_CHUNK] += part


def _final(rows, rb, mod_ref, x1_s, acc_s, fg_ref, store_y_rows):
    for r0 in range(0, rows, ROW_CHUNK):
        g2 = _mod_rows(mod_ref, 5, r0, rb)
        x2 = x1_s[r0:r0 + ROW_CHUNK, :] + g2 * acc_s[r0:r0 + ROW_CHUNK, :]
        store_y_rows(r0, _rms(x2) * fg_ref[...])


def _prompt_kernel(x_ref, mod_ref, w_in, w_co, pmix, w_po, w_out, w_f1, w_f2,
                   cw_ref, cb_ref, lg_ref, lb_ref, ps_ref, fg_ref,
                   y_ref, ncp_ref, npp_ref,
                   xrow, yrow, h_s, h2_s, a_ext, u_ext, cn_s, pl_s, pm_s, m_s, x1_s, acc_s, hid_s):
    j = pl.program_id(0)
    n_tiles = pl.num_programs(0) - 1
    a_h = CONV_HIST * BATCH
    u_h = POOL_HIST * BATCH

    @pl.when(j == 0)
    def _():
        a_ext[0:a_h, :] = jnp.zeros((a_h, D), f32)
        u_ext[0:u_h, :] = jnp.zeros((u_h, D), f32)
        h2_s[...] = jnp.zeros((TQ, D), bf16)
        x1_s[...] = jnp.zeros((TQ, D), f32)

    def store_y_rows(r0, y):
        for g in range(N_LANE_GROUPS):
            yrow[g, r0:r0 + ROW_CHUNK, :] = y[:, g * LANES:(g + 1) * LANES]

    def deinterleave(dst_ref, n_t):
        for b in range(BATCH):
            for t8 in range(0, n_t, SUBLANES):
                for g in range(N_LANE_GROUPS):
                    dst_ref[b, t8:t8 + SUBLANES, g * LANES:(g + 1) * LANES] = (
                        yrow[g, pl.ds(t8 * BATCH + b, SUBLANES, stride=BATCH), :])

    @pl.when(j < n_tiles)
    def _():
        for b in range(BATCH):
            for t8 in range(0, TT, SUBLANES):
                for g in range(N_LANE_GROUPS):
                    xrow[g, pl.ds(t8 * BATCH + b, SUBLANES, stride=BATCH), :] = (
                        x_ref[b, t8:t8 + SUBLANES, g * LANES:(g + 1) * LANES])

        def load_x_rows(r0):
            return jnp.concatenate([xrow[g, r0:r0 + ROW_CHUNK, :] for g in range(N_LANE_GROUPS)], axis=1)

        def load_x_cols(c0):
            g0 = c0 // LANES
            return jnp.concatenate([xrow[g0 + i] for i in range(COL_CHUNK // LANES)], axis=1)

        _norm_mod(load_x_rows, h_s, mod_ref, 0, 1, TQ, BATCH)
        _glu_and_pool_in(h_s, w_in, a_ext.at[a_h:a_h + TQ], u_ext.at[u_h:u_h + TQ])

        _mlp(h2_s, hid_s, acc_s, w_f1, w_f2)

        first = a_h - (CONV_W - 1) * BATCH
        for r0 in range(0, TQ, ROW_CHUNK):
            parts = []
            for c0 in range(0, D, COL_CHUNK):
                acc = jnp.zeros((ROW_CHUNK, COL_CHUNK), f32)
                for k in range(CONV_W):
                    rk = r0 + first + k * BATCH
                    acc = acc + a_ext[rk:rk + ROW_CHUNK, c0:c0 + COL_CHUNK] * cw_ref[k:k + 1, c0:c0 + COL_CHUNK]
                parts.append(acc)
            cf = jnp.concatenate(parts, axis=1) + cb_ref[...]
            cn_s[r0:r0 + ROW_CHUNK, :] = _ln_silu(cf, lg_ref, lb_ref).astype(bf16)
            t_idx = j * TT + r0 // BATCH + lax.broadcasted_iota(jnp.int32, (ROW_CHUNK, 1), 0) // BATCH
            parts = []
            for gi, w in enumerate(POOL_WINDOWS):
                c0 = gi * POOL_GROUP
                acc = jnp.zeros((ROW_CHUNK, POOL_GROUP), f32)
                for jj in range(w):
                    rk = r0 + u_h - jj * BATCH
                    acc = acc + u_ext[rk:rk + ROW_CHUNK, c0:c0 + POOL_GROUP]
                inv_cnt = 1.0 / jnp.minimum(t_idx + 1, w).astype(f32)
                parts.append(acc * inv_cnt - u_ext[r0 + u_h:r0 + u_h + ROW_CHUNK, c0:c0 + POOL_GROUP])
            pl_s[r0:r0 + ROW_CHUNK, :] = jnp.concatenate(parts, axis=1).astype(bf16)

        _final(TQ, BATCH, mod_ref, x1_s, acc_s, fg_ref, store_y_rows)
        deinterleave(y_ref, TT)

        _merge(TQ, BATCH, load_x_cols, mod_ref, h_s, h2_s, cn_s, pl_s, pm_s, m_s, x1_s,
               w_in, w_co, pmix, w_po, w_out, ps_ref)

        a_ext[0:a_h, :] = a_ext[TQ:TQ + a_h, :]
        u_ext[0:u_h, :] = u_ext[TQ:TQ + u_h, :]

    @pl.when(j == n_tiles)
    def _():
        _mlp(h2_s, hid_s, acc_s, w_f1, w_f2)
        _final(TQ, BATCH, mod_ref, x1_s, acc_s, fg_ref, store_y_rows)
        deinterleave(y_ref, TT)

    @pl.when(j == n_tiles - 1)
    def _():
        for g in range(N_LANE_GROUPS):
            yrow[g, 0:a_h, :] = a_ext[0:a_h, g * LANES:(g + 1) * LANES]
        deinterleave(ncp_ref, CONV_HIST)
        for g in range(N_LANE_GROUPS):
            yrow[g, 0:u_h, :] = u_ext[0:u_h, g * LANES:(g + 1) * LANES]
        deinterleave(npp_ref, POOL_HIST)


def _weight_specs():
    return [
        _const_spec((D, 5 * D)),
        _const_spec((D, D)),
        _const_spec((len(POOL_WINDOWS), POOL_GROUP, POOL_GROUP)),
        _const_spec((D, D)),
        _const_spec((D, D)),
        _const_spec((D, D_FF)),
        _const_spec((D_FF, D)),
        _const_spec((CONV_W, D)),
        _const_spec((1, D)),
        _const_spec((1, D)),
        _const_spec((1, D)),
        _const_spec((1, D)),
        _const_spec((1, D)),
    ]


def _prompt_layer(x_prompt, mod, weights):
    n_tiles = SEQ // TT
    return pl.pallas_call(
        _prompt_kernel,
        grid=(n_tiles + 1,),
        in_specs=[
            pl.BlockSpec((BATCH, TT, D), lambda j: (0, jnp.minimum(j, n_tiles - 1), 0)),
            pl.BlockSpec((N_MOD, BATCH, D), lambda j: (0, DEC_BATCH // BATCH, 0)),
        ] + _weight_specs(),
        out_specs=[
            pl.BlockSpec((BATCH, TT, D), lambda j: (0, jnp.maximum(j - 1, 0), 0)),
            pl.BlockSpec((BATCH, CONV_HIST, D), lambda j: (0, 0, 0)),
            pl.BlockSpec((BATCH, POOL_HIST, D), lambda j: (0, 0, 0)),
        ],
        out_shape=[
            jax.ShapeDtypeStruct((BATCH, SEQ, D), f32),
            jax.ShapeDtypeStruct((BATCH, CONV_HIST, D), f32),
            jax.ShapeDtypeStruct((BATCH, POOL_HIST, D), f32),
        ],
        scratch_shapes=[
            pltpu.VMEM((N_LANE_GROUPS, TQ, LANES), f32),
            pltpu.VMEM((N_LANE_GROUPS, TQ, LANES), f32),
            pltpu.VMEM((TQ, D), bf16),
            pltpu.VMEM((TQ, D), bf16),
            pltpu.VMEM(((CONV_HIST + TT) * BATCH, D), f32),
            pltpu.VMEM(((POOL_HIST + TT) * BATCH, D), f32),
            pltpu.VMEM((TQ, D), bf16),
            pltpu.VMEM((TQ, D), bf16),
            pltpu.VMEM((TQ, D), bf16),
            pltpu.VMEM((TQ, D), bf16),
            pltpu.VMEM((TQ, D), f32),
            pltpu.VMEM((TQ, D), f32),
            pltpu.VMEM((TQ, D), bf16),
        ],
        compiler_params=pltpu.CompilerParams(
            dimension_semantics=("arbitrary",), vmem_limit_bytes=VMEM_LIMIT),
        name="prompt_layer",
    )(x_prompt, mod, *weights)


HIST_ROWS = 16


def _hist_kernel(sc_ref, sp_ref, wt_ref, hc_ref, hp_ref):
    c_time = lax.broadcasted_iota(jnp.int32, (CONV_W - 1, D), 0)
    p_time = lax.broadcasted_iota(jnp.int32, (MAX_POOL - 1, D), 0)
    lane = lax.broadcasted_iota(jnp.int32, (MAX_POOL - 1, D), 1)
    window = jnp.full((MAX_POOL - 1, D), POOL_WINDOWS[0], jnp.int32)
    for gi, w in enumerate(POOL_WINDOWS):
        window = jnp.where(lane >= gi * POOL_GROUP, w, window)
    for s in range(HIST_ROWS):
        sc = sc_ref[s]
        sp = sp_ref[s]
        for t in range(DEC_SEQ):
            hc_ref[t, s:s + 1, :] = jnp.sum(
                jnp.where(c_time >= t, sc, 0.0) * wt_ref[t], axis=0, keepdims=True)
            hp_ref[t, s:s + 1, :] = jnp.sum(
                jnp.where(p_time >= MAX_POOL - window + t, sp, 0.0), axis=0, keepdims=True)


def _sample_hist(sc, sp, conv_w):
    taps = jnp.stack([jnp.pad(conv_w[:CONV_W - 1 - t], ((t, 0), (0, 0))) for t in range(DEC_SEQ)])
    return pl.pallas_call(
        _hist_kernel,
        grid=(DEC_BATCH // HIST_ROWS,),
        in_specs=[
            pl.BlockSpec((HIST_ROWS, CONV_W - 1, D), lambda i: (i, 0, 0)),
            pl.BlockSpec((HIST_ROWS, MAX_POOL - 1, D), lambda i: (i, 0, 0)),
            pl.BlockSpec((DEC_SEQ, CONV_W - 1, D), lambda i: (0, 0, 0)),
        ],
        out_specs=[
            pl.BlockSpec((DEC_SEQ, HIST_ROWS, D), lambda i: (0, i, 0)),
            pl.BlockSpec((DEC_SEQ, HIST_ROWS, D), lambda i: (0, i, 0)),
        ],
        out_shape=[
            jax.ShapeDtypeStruct((DEC_SEQ, DEC_BATCH, D), f32),
            jax.ShapeDtypeStruct((DEC_SEQ, DEC_BATCH, D), f32),
        ],
        name="sample_hist",
    )(sc, sp, taps)


def _sample_kernel(x_ref, hc_ref, hp_ref, mod_ref, w_in, w_co, pmix, w_po, w_out, w_f1, w_f2,
                   cw_ref, cb_ref, lg_ref, lb_ref, ps_ref, fg_ref,
                   y_ref, an_ref, un_ref,
                   a_all, u_all, h_s, h2_s, cn_s, pl_s, pm_s, m_s, x1_s, acc_s, hid_s):
    t = pl.program_id(0)
    rows = DEC_BATCH

    _norm_mod(lambda r0: x_ref[0, r0:r0 + ROW_CHUNK, :], h_s, mod_ref, 0, 1, rows, rows)
    _glu_and_pool_in(h_s, w_in, a_all.at[t], u_all.at[t])
    an_ref[0] = a_all[t]
    un_ref[0] = u_all[t]

    for r0 in range(0, rows, ROW_CHUNK):
        cf = hc_ref[0, r0:r0 + ROW_CHUNK, :] + cb_ref[...]
        for d in range(DEC_SEQ):
            prev = a_all[jnp.maximum(t - d, 0), r0:r0 + ROW_CHUNK, :]
            k = CONV_W - 1 - d
            cf = cf + jnp.where(t >= d, prev * cw_ref[k:k + 1, :], 0.0)
        cn_s[r0:r0 + ROW_CHUNK, :] = _ln_silu(cf, lg_ref, lb_ref).astype(bf16)
        parts = []
        for gi, w in enumerate(POOL_WINDOWS):
            c0 = gi * POOL_GROUP
            acc = hp_ref[0, r0:r0 + ROW_CHUNK, c0:c0 + POOL_GROUP]
            for d in range(min(w, DEC_SEQ)):
                prev = u_all[jnp.maximum(t - d, 0), r0:r0 + ROW_CHUNK, c0:c0 + POOL_GROUP]
                acc = acc + jnp.where(t >= d, prev, 0.0)
            cnt = jnp.minimum(jnp.full((ROW_CHUNK, 1), PAST_LEN + 1, jnp.int32) + t, w).astype(f32)
            parts.append(acc * (1.0 / cnt) - u_all[t, r0:r0 + ROW_CHUNK, c0:c0 + POOL_GROUP])
        pl_s[r0:r0 + ROW_CHUNK, :] = jnp.concatenate(parts, axis=1).astype(bf16)

    def load_x_cols(c0):
        return x_ref[0, :, c0:c0 + COL_CHUNK]

    def store_y_rows(r0, y):
        y_ref[0, r0:r0 + ROW_CHUNK, :] = y

    _merge(rows, rows, load_x_cols, mod_ref, h_s, h2_s, cn_s, pl_s, pm_s, m_s, x1_s,
           w_in, w_co, pmix, w_po, w_out, ps_ref)
    _mlp(h2_s, hid_s, acc_s, w_f1, w_f2)
    _final(rows, rows, mod_ref, x1_s, acc_s, fg_ref, store_y_rows)


def _sample_layer(xs_t, hc, hp, mod, weights):
    slab = lambda: pl.BlockSpec((1, DEC_BATCH, D), lambda t: (t, 0, 0))
    return pl.pallas_call(
        _sample_kernel,
        grid=(DEC_SEQ,),
        in_specs=[slab(), slab(), slab(),
                  pl.BlockSpec((N_MOD, DEC_BATCH, D), lambda t: (0, 0, 0))] + _weight_specs(),
        out_specs=[slab(), slab(), slab()],
        out_shape=[jax.ShapeDtypeStruct((DEC_SEQ, DEC_BATCH, D), f32)] * 3,
        scratch_shapes=[
            pltpu.VMEM((DEC_SEQ, DEC_BATCH, D), f32),
            pltpu.VMEM((DEC_SEQ, DEC_BATCH, D), f32),
            pltpu.VMEM((DEC_BATCH, D), bf16),
            pltpu.VMEM((DEC_BATCH, D), bf16),
            pltpu.VMEM((DEC_BATCH, D), bf16),
            pltpu.VMEM((DEC_BATCH, D), bf16),
            pltpu.VMEM((DEC_BATCH, D), bf16),
            pltpu.VMEM((DEC_BATCH, D), bf16),
            pltpu.VMEM((DEC_BATCH, D), f32),
            pltpu.VMEM((DEC_BATCH, D), f32),
            pltpu.VMEM((DEC_BATCH, D), bf16),
        ],
        compiler_params=pltpu.CompilerParams(
            dimension_semantics=("arbitrary",), vmem_limit_bytes=VMEM_LIMIT),
        name="sample_layer",
    )(xs_t, hc, hp, mod, *weights)


def kernel(x_prompt, x_sample, state_conv, state_pool, c_prompt, c_sample, w_ada, b_ada, w_in,
           conv_w, conv_b, ln_g, ln_b, w_conv_out, pool_mix, pool_scale, w_pool_out, w_out,
           w_ff1, w_ff2, final_g):
    assert w_in.shape[0] == 1, "single layer"
    mod = _ada(c_sample, c_prompt, w_ada[0], b_ada[0])
    weights = (
        w_in[0].astype(bf16), w_conv_out[0].astype(bf16), pool_mix[0].astype(bf16),
        w_pool_out[0].astype(bf16), w_out[0].astype(bf16), w_ff1[0].astype(bf16), w_ff2[0].astype(bf16),
        conv_w[0], conv_b, ln_g, ln_b, pool_scale, final_g.reshape(1, D),
    )
    y_prompt, ncp, npp = _prompt_layer(x_prompt, mod, weights)
    new_conv_prompt = ncp[None, :, CONV_HIST - (CONV_W - 1):, :]
    new_pool_prompt = npp[None, :, POOL_HIST - (MAX_POOL - 1):, :]

    sc = state_conv[0]
    sp = state_pool[0]
    hc, hp = _sample_hist(sc, sp, conv_w[0])
    ys_t, a_new, u_new = _sample_layer(x_sample.transpose(1, 0, 2), hc, hp, mod, weights)
    y_sample = ys_t.transpose(1, 0, 2)
    new_conv_sample = jnp.concatenate([sc[:, DEC_SEQ:], a_new.transpose(1, 0, 2)], axis=1)[None]
    new_pool_sample = jnp.concatenate([sp[:, DEC_SEQ:], u_new.transpose(1, 0, 2)], axis=1)[None]
    return (y_prompt, y_sample, new_conv_prompt, new_pool_prompt, new_conv_sample, new_pool_sample)
```

```python
import jax
import jax.numpy as jnp
from jax import lax
from jax.experimental import pallas as pl
from jax.experimental.pallas import tpu as pltpu

D = 1024
BATCH = 8
SEQ = 2048
DEC_BATCH = 128
DEC_SEQ = 4
PAST_LEN = 16384
CONV_W = 31
POOL_WINDOWS = (2, 4, 8, 16)
POOL_GROUP = D // len(POOL_WINDOWS)
MAX_POOL = max(POOL_WINDOWS)
D_FF = 4 * D
N_MOD = 6
EPS = 1e-6

SUBLANES = 8
LANES = 128
N_LANE_GROUPS = D // LANES
ROW_CHUNK = 32
COL_CHUNK = 256
TT = 32
TQ = TT * BATCH
CONV_HIST = 32
POOL_HIST = 16
VMEM_LIMIT = 58 * 1024 * 1024

f32 = jnp.float32
bf16 = jnp.bfloat16


def _dot(a, b):
    return jnp.dot(a, b, preferred_element_type=f32)


def _sigmoid(x):
    return 1.0 / (1.0 + jnp.exp(-x))


def _rms(x):
    return x * lax.rsqrt(jnp.mean(x * x, axis=-1, keepdims=True) + EPS)


def _const_spec(shape):
    nd = len(shape)
    return pl.BlockSpec(shape, lambda *_: (0,) * nd, pipeline_mode=pl.Buffered(1))


def _ada_kernel(cs_ref, cp_ref, w_ref, b_ref, o_ref):
    w = w_ref[...].astype(bf16)
    b = b_ref[0]
    cs = cs_ref[...]
    cp = cp_ref[...]
    o_ref[0, 0:DEC_BATCH, :] = _dot((cs * _sigmoid(cs)).astype(bf16), w) + b
    o_ref[0, DEC_BATCH:DEC_BATCH + BATCH, :] = _dot((cp * _sigmoid(cp)).astype(bf16), w) + b


def _ada(c_sample, c_prompt, w_ada, b_ada):
    return pl.pallas_call(
        _ada_kernel,
        grid=(N_MOD,),
        in_specs=[
            pl.BlockSpec((DEC_BATCH, D), lambda i: (0, 0)),
            pl.BlockSpec((BATCH, D), lambda i: (0, 0)),
            pl.BlockSpec((D, D), lambda i: (0, i)),
            pl.BlockSpec((1, 1, D), lambda i: (i, 0, 0)),
        ],
        out_specs=pl.BlockSpec((1, DEC_BATCH + BATCH, D), lambda i: (i, 0, 0)),
        out_shape=jax.ShapeDtypeStruct((N_MOD, DEC_BATCH + BATCH, D), f32),
        name="ada_mod",
    )(c_sample, c_prompt, w_ada, b_ada.reshape(N_MOD, 1, D))


def _mod_rows(mod_ref, i, r0, rb):
    if rb < ROW_CHUNK:
        m = mod_ref[i]
        return jnp.concatenate([m] * (ROW_CHUNK // rb), axis=0)
    return mod_ref[i, r0 % rb:r0 % rb + ROW_CHUNK, :]


def _mod_cols(mod_ref, i, c0, rows, rb):
    m = mod_ref[i, :, c0:c0 + COL_CHUNK]
    if rb < rows:
        return jnp.concatenate([m] * (rows // rb), axis=0)
    return m


def _glu_and_pool_in(h_s, w_in, a_dst, u_dst):
    h = h_s[...]
    for c0 in range(0, D, COL_CHUNK):
        val = _dot(h, w_in[:, c0:c0 + COL_CHUNK])
        gate = _dot(h, w_in[:, D + c0:D + c0 + COL_CHUNK])
        a_dst[:, c0:c0 + COL_CHUNK] = val * _sigmoid(gate)
        u_dst[:, c0:c0 + COL_CHUNK] = _dot(h, w_in[:, 2 * D + c0:2 * D + c0 + COL_CHUNK])


def _ln_silu(cf, lg_ref, lb_ref):
    mu = jnp.mean(cf, axis=-1, keepdims=True)
    cc = cf - mu
    var = jnp.mean(cc * cc, axis=-1, keepdims=True)
    cn = cc * lax.rsqrt(var + EPS) * lg_ref[...] + lb_ref[...]
    return cn * _sigmoid(cn)


def _norm_mod(load_rows, dst, mod_ref, i_shift, i_scale, rows, rb):
    for r0 in range(0, rows, ROW_CHUNK):
        x = load_rows(r0)
        shift = _mod_rows(mod_ref, i_shift, r0, rb)
        scale = _mod_rows(mod_ref, i_scale, r0, rb)
        dst[r0:r0 + ROW_CHUNK, :] = (_rms(x) * (1.0 + scale) + shift).astype(bf16)


def _merge(rows, rb, load_x_cols, mod_ref, h_s, h2_s, cn_s, pl_s, pm_s, m_s, x1_s,
           w_in, w_co, pmix, w_po, w_out, ps_ref):
    for g in range(len(POOL_WINDOWS)):
        c0 = g * POOL_GROUP
        pm = _dot(pl_s[:, c0:c0 + POOL_GROUP], pmix[g]) * ps_ref[:, c0:c0 + POOL_GROUP]
        pm_s[:, c0:c0 + POOL_GROUP] = pm.astype(bf16)
    h = h_s[...]
    cn = cn_s[...]
    pm = pm_s[...]
    for c0 in range(0, D, COL_CHUNK):
        ga = _dot(h, w_in[:, 3 * D + c0:3 * D + c0 + COL_CHUNK])
        gb = _dot(h, w_in[:, 4 * D + c0:4 * D + c0 + COL_CHUNK])
        o_a = _dot(cn, w_co[:, c0:c0 + COL_CHUNK])
        o_b = _dot(pm, w_po[:, c0:c0 + COL_CHUNK])
        m_s[:, c0:c0 + COL_CHUNK] = (_sigmoid(ga) * o_a + _sigmoid(gb) * o_b).astype(bf16)
    m = m_s[...]
    for c0 in range(0, D, COL_CHUNK):
        g1 = _mod_cols(mod_ref, 2, c0, rows, rb)
        x1_s[:, c0:c0 + COL_CHUNK] = load_x_cols(c0) + g1 * _dot(m, w_out[:, c0:c0 + COL_CHUNK])
    _norm_mod(lambda r0: x1_s[r0:r0 + ROW_CHUNK, :], h2_s, mod_ref, 3, 4, rows, rb)


def _mlp(h2_s, hid_s, acc_s, w_f1, w_f2):
    h2 = h2_s[...]
    for f0 in range(0, D_FF, D):
        for c0 in range(0, D, COL_CHUNK):
            hid = jnp.maximum(_dot(h2, w_f1[:, f0 + c0:f0 + c0 + COL_CHUNK]), 0.0)
            hid_s[:, c0:c0 + COL_CHUNK] = (hid * hid).astype(bf16)
        hid = hid_s[...]
        for c0 in range(0, D, COL_CHUNK):
            part = _dot(hid, w_f2[f0:f0 + D, c0:c0 + COL_CHUNK])
            if f0 == 0:
                acc_s[:, c0:c0 + COL_CHUNK] = part
            else:
                acc_s[:, c0:c0 + COL_CHUNK] += part


def _final(rows, rb, mod_ref, x1_s, acc_s, fg_ref, store_y_rows):
    for r0 in range(0, rows, ROW_CHUNK):
        g2 = _mod_rows(mod_ref, 5, r0, rb)
        x2 = x1_s[r0:r0 + ROW_CHUNK, :] + g2 * acc_s[r0:r0 + ROW_CHUNK, :]
        store_y_rows(r0, _rms(x2) * fg_ref[...])


def _prompt_kernel(x_ref, mod_ref, w_in, w_co, pmix, w_po, w_out, w_f1, w_f2,
                   cw_ref, cb_ref, lg_ref, lb_ref, ps_ref, fg_ref,
                   y_ref, ncp_ref, npp_ref,
                   xrow, yrow, h_s, h2_s, a_ext, u_ext, cn_s, pl_s, pm_s, m_s, x1_s, acc_s, hid_s):
    j = pl.program_id(0)
    n_tiles = pl.num_programs(0) - 1
    a_h = CONV_HIST * BATCH
    u_h = POOL_HIST * BATCH

    def store_y_rows(r0, y):
        for g in range(N_LANE_GROUPS):
            yrow[g, r0:r0 + ROW_CHUNK, :] = y[:, g * LANES:(g + 1) * LANES]

    def deinterleave(dst_ref, n_t):
        for b in range(BATCH):
            for t8 in range(0, n_t, SUBLANES):
                for g in range(N_LANE_GROUPS):
                    dst_ref[b, t8:t8 + SUBLANES, g * LANES:(g + 1) * LANES] = (
                        yrow[g, pl.ds(t8 * BATCH + b, SUBLANES, stride=BATCH), :])

    def mixer_step(with_mlp):
        for b in range(BATCH):
            for t8 in range(0, TT, SUBLANES):
                for g in range(N_LANE_GROUPS):
                    xrow[g, pl.ds(t8 * BATCH + b, SUBLANES, stride=BATCH), :] = (
                        x_ref[b, t8:t8 + SUBLANES, g * LANES:(g + 1) * LANES])

        def load_x_rows(r0):
            return jnp.concatenate([xrow[g, r0:r0 + ROW_CHUNK, :] for g in range(N_LANE_GROUPS)], axis=1)

        def load_x_cols(c0):
            g0 = c0 // LANES
            return jnp.concatenate([xrow[g0 + i] for i in range(COL_CHUNK // LANES)], axis=1)

        _norm_mod(load_x_rows, h_s, mod_ref, 0, 1, TQ, BATCH)
        _glu_and_pool_in(h_s, w_in, a_ext.at[a_h:a_h + TQ], u_ext.at[u_h:u_h + TQ])

        if with_mlp:
            _mlp(h2_s, hid_s, acc_s, w_f1, w_f2)

        first = a_h - (CONV_W - 1) * BATCH
        for r0 in range(0, TQ, ROW_CHUNK):
            parts = []
            for c0 in range(0, D, COL_CHUNK):
                acc = jnp.zeros((ROW_CHUNK, COL_CHUNK), f32)
                for k in range(CONV_W):
                    rk = r0 + first + k * BATCH
                    acc = acc + a_ext[rk:rk + ROW_CHUNK, c0:c0 + COL_CHUNK] * cw_ref[k:k + 1, c0:c0 + COL_CHUNK]
                parts.append(acc)
            cf = jnp.concatenate(parts, axis=1) + cb_ref[...]
            cn_s[r0:r0 + ROW_CHUNK, :] = _ln_silu(cf, lg_ref, lb_ref).astype(bf16)
            t_idx = j * TT + r0 // BATCH + lax.broadcasted_iota(jnp.int32, (ROW_CHUNK, 1), 0) // BATCH
            parts = []
            for gi, w in enumerate(POOL_WINDOWS):
                c0 = gi * POOL_GROUP
                acc = jnp.zeros((ROW_CHUNK, POOL_GROUP), f32)
                for jj in range(w):
                    rk = r0 + u_h - jj * BATCH
                    acc = acc + u_ext[rk:rk + ROW_CHUNK, c0:c0 + POOL_GROUP]
                inv_cnt = 1.0 / jnp.minimum(t_idx + 1, w).astype(f32)
                parts.append(acc * inv_cnt - u_ext[r0 + u_h:r0 + u_h + ROW_CHUNK, c0:c0 + POOL_GROUP])
            pl_s[r0:r0 + ROW_CHUNK, :] = jnp.concatenate(parts, axis=1).astype(bf16)

        if with_mlp:
            _final(TQ, BATCH, mod_ref, x1_s, acc_s, fg_ref, store_y_rows)
            deinterleave(y_ref, TT)

        _merge(TQ, BATCH, load_x_cols, mod_ref, h_s, h2_s, cn_s, pl_s, pm_s, m_s, x1_s,
               w_in, w_co, pmix, w_po, w_out, ps_ref)

        a_ext[0:a_h, :] = a_ext[TQ:TQ + a_h, :]
        u_ext[0:u_h, :] = u_ext[TQ:TQ + u_h, :]

    @pl.when(j == 0)
    def _():
        a_ext[0:a_h, :] = jnp.zeros((a_h, D), f32)
        u_ext[0:u_h, :] = jnp.zeros((u_h, D), f32)
        mixer_step(False)

    @pl.when((j > 0) & (j < n_tiles))
    def _():
        mixer_step(True)

    @pl.when(j == n_tiles)
    def _():
        _mlp(h2_s, hid_s, acc_s, w_f1, w_f2)
        _final(TQ, BATCH, mod_ref, x1_s, acc_s, fg_ref, store_y_rows)
        deinterleave(y_ref, TT)

    @pl.when(j == n_tiles - 1)
    def _():
        for g in range(N_LANE_GROUPS):
            yrow[g, 0:a_h, :] = a_ext[0:a_h, g * LANES:(g + 1) * LANES]
        deinterleave(ncp_ref, CONV_HIST)
        for g in range(N_LANE_GROUPS):
            yrow[g, 0:u_h, :] = u_ext[0:u_h, g * LANES:(g + 1) * LANES]
        deinterleave(npp_ref, POOL_HIST)


def _weight_specs():
    return [
        _const_spec((D, 5 * D)),
        _const_spec((D, D)),
        _const_spec((len(POOL_WINDOWS), POOL_GROUP, POOL_GROUP)),
        _const_spec((D, D)),
        _const_spec((D, D)),
        _const_spec((D, D_FF)),
        _const_spec((D_FF, D)),
        _const_spec((CONV_W, D)),
        _const_spec((1, D)),
        _const_spec((1, D)),
        _const_spec((1, D)),
        _const_spec((1, D)),
        _const_spec((1, D)),
    ]


def _prompt_layer(x_prompt, mod, weights):
    n_tiles = SEQ // TT
    return pl.pallas_call(
        _prompt_kernel,
        grid=(n_tiles + 1,),
        in_specs=[
            pl.BlockSpec((BATCH, TT, D), lambda j: (0, jnp.minimum(j, n_tiles - 1), 0)),
            pl.BlockSpec((N_MOD, BATCH, D), lambda j: (0, DEC_BATCH // BATCH, 0)),
        ] + _weight_specs(),
        out_specs=[
            pl.BlockSpec((BATCH, TT, D), lambda j: (0, jnp.maximum(j - 1, 0), 0)),
            pl.BlockSpec((BATCH, CONV_HIST, D), lambda j: (0, 0, 0)),
            pl.BlockSpec((BATCH, POOL_HIST, D), lambda j: (0, 0, 0)),
        ],
        out_shape=[
            jax.ShapeDtypeStruct((BATCH, SEQ, D), f32),
            jax.ShapeDtypeStruct((BATCH, CONV_HIST, D), f32),
            jax.ShapeDtypeStruct((BATCH, POOL_HIST, D), f32),
        ],
        scratch_shapes=[
            pltpu.VMEM((N_LANE_GROUPS, TQ, LANES), f32),
            pltpu.VMEM((N_LANE_GROUPS, TQ, LANES), f32),
            pltpu.VMEM((TQ, D), bf16),
            pltpu.VMEM((TQ, D), bf16),
            pltpu.VMEM(((CONV_HIST + TT) * BATCH, D), f32),
            pltpu.VMEM(((POOL_HIST + TT) * BATCH, D), f32),
            pltpu.VMEM((TQ, D), bf16),
            pltpu.VMEM((TQ, D), bf16),
            pltpu.VMEM((TQ, D), bf16),
            pltpu.VMEM((TQ, D), bf16),
            pltpu.VMEM((TQ, D), f32),
            pltpu.VMEM((TQ, D), f32),
            pltpu.VMEM((TQ, D), bf16),
        ],
        compiler_params=pltpu.CompilerParams(
            dimension_semantics=("arbitrary",), vmem_limit_bytes=VMEM_LIMIT),
        name="prompt_layer",
    )(x_prompt, mod, *weights)


HIST_ROWS = 16


def _hist_kernel(sc_ref, sp_ref, cw_ref, hc_ref, hp_ref):
    for t in range(DEC_SEQ):
        for c0 in range(0, D, COL_CHUNK):
            acc = jnp.zeros((HIST_ROWS, COL_CHUNK), f32)
            for jh in range(t, CONV_W - 1):
                acc = acc + sc_ref[jh, :, c0:c0 + COL_CHUNK] * cw_ref[jh - t:jh - t + 1, c0:c0 + COL_CHUNK]
            hc_ref[t, :, c0:c0 + COL_CHUNK] = acc
        for gi, w in enumerate(POOL_WINDOWS):
            c0 = gi * POOL_GROUP
            acc = jnp.zeros((HIST_ROWS, POOL_GROUP), f32)
            for jj in range(t + 1, w):
                acc = acc + sp_ref[MAX_POOL - 1 + t - jj, :, c0:c0 + POOL_GROUP]
            hp_ref[t, :, c0:c0 + POOL_GROUP] = acc


def _sample_hist(sc_t, sp_t, conv_w):
    return pl.pallas_call(
        _hist_kernel,
        grid=(DEC_BATCH // HIST_ROWS,),
        in_specs=[
            pl.BlockSpec((CONV_W - 1, HIST_ROWS, D), lambda i: (0, i, 0)),
            pl.BlockSpec((MAX_POOL - 1, HIST_ROWS, D), lambda i: (0, i, 0)),
            pl.BlockSpec((CONV_W, D), lambda i: (0, 0)),
        ],
        out_specs=[
            pl.BlockSpec((DEC_SEQ, HIST_ROWS, D), lambda i: (0, i, 0)),
            pl.BlockSpec((DEC_SEQ, HIST_ROWS, D), lambda i: (0, i, 0)),
        ],
        out_shape=[
            jax.ShapeDtypeStruct((DEC_SEQ, DEC_BATCH, D), f32),
            jax.ShapeDtypeStruct((DEC_SEQ, DEC_BATCH, D), f32),
        ],
        name="sample_hist",
    )(sc_t, sp_t, conv_w)


SAMPLE_TT = 2
SAMPLE_ROWS = SAMPLE_TT * DEC_BATCH


def _sample_kernel(x_ref, hc_ref, hp_ref, mod_ref, w_in, w_co, pmix, w_po, w_out, w_f1, w_f2,
                   cw_ref, cb_ref, lg_ref, lb_ref, ps_ref, fg_ref,
                   y_ref, an_ref, un_ref,
                   a_all, u_all, h_s, h2_s, cn_s, pl_s, pm_s, m_s, x1_s, acc_s, hid_s):
    t0 = pl.program_id(0) * SAMPLE_TT
    rows = SAMPLE_ROWS

    def load_x_rows(q0):
        return x_ref[q0 // DEC_BATCH, q0 % DEC_BATCH:q0 % DEC_BATCH + ROW_CHUNK, :]

    _norm_mod(load_x_rows, h_s, mod_ref, 0, 1, rows, DEC_BATCH)
    _glu_and_pool_in(h_s, w_in, an_ref, un_ref)
    for tl in range(SAMPLE_TT):
        a_all[t0 + tl] = an_ref[tl * DEC_BATCH:(tl + 1) * DEC_BATCH, :]
        u_all[t0 + tl] = un_ref[tl * DEC_BATCH:(tl + 1) * DEC_BATCH, :]

    for tl in range(SAMPLE_TT):
        t = t0 + tl
        for r0 in range(0, DEC_BATCH, ROW_CHUNK):
            q0 = tl * DEC_BATCH + r0
            cf = hc_ref[tl, r0:r0 + ROW_CHUNK, :] + cb_ref[...]
            for d in range(DEC_SEQ):
                prev = a_all[jnp.maximum(t - d, 0), r0:r0 + ROW_CHUNK, :]
                k = CONV_W - 1 - d
                cf = cf + jnp.where(t >= d, prev * cw_ref[k:k + 1, :], 0.0)
            cn_s[q0:q0 + ROW_CHUNK, :] = _ln_silu(cf, lg_ref, lb_ref).astype(bf16)
            parts = []
            for gi, w in enumerate(POOL_WINDOWS):
                c0 = gi * POOL_GROUP
                acc = hp_ref[tl, r0:r0 + ROW_CHUNK, c0:c0 + POOL_GROUP]
                for d in range(min(w, DEC_SEQ)):
                    prev = u_all[jnp.maximum(t - d, 0), r0:r0 + ROW_CHUNK, c0:c0 + POOL_GROUP]
                    acc = acc + jnp.where(t >= d, prev, 0.0)
                cnt = jnp.minimum(jnp.full((ROW_CHUNK, 1), PAST_LEN + 1, jnp.int32) + t, w).astype(f32)
                parts.append(acc * (1.0 / cnt) - u_all[t, r0:r0 + ROW_CHUNK, c0:c0 + POOL_GROUP])
            pl_s[q0:q0 + ROW_CHUNK, :] = jnp.concatenate(parts, axis=1).astype(bf16)

    def load_x_cols(c0):
        return jnp.concatenate([x_ref[tl, :, c0:c0 + COL_CHUNK] for tl in range(SAMPLE_TT)], axis=0)

    def store_y_rows(q0, y):
        y_ref[q0 // DEC_BATCH, q0 % DEC_BATCH:q0 % DEC_BATCH + ROW_CHUNK, :] = y

    _merge(rows, DEC_BATCH, load_x_cols, mod_ref, h_s, h2_s, cn_s, pl_s, pm_s, m_s, x1_s,
           w_in, w_co, pmix, w_po, w_out, ps_ref)
    _mlp(h2_s, hid_s, acc_s, w_f1, w_f2)
    _final(rows, DEC_BATCH, mod_ref, x1_s, acc_s, fg_ref, store_y_rows)


def _sample_layer(xs_t, hc, hp, mod, weights):
    slab = lambda: pl.BlockSpec((SAMPLE_TT, DEC_BATCH, D), lambda p: (p, 0, 0))
    rows = lambda: pl.BlockSpec((SAMPLE_ROWS, D), lambda p: (p, 0))
    return pl.pallas_call(
        _sample_kernel,
        grid=(DEC_SEQ // SAMPLE_TT,),
        in_specs=[slab(), slab(), slab(), _const_spec((N_MOD, DEC_BATCH, D))] + _weight_specs(),
        out_specs=[slab(), rows(), rows()],
        out_shape=[jax.ShapeDtypeStruct((DEC_SEQ, DEC_BATCH, D), f32),
                   jax.ShapeDtypeStruct((DEC_SEQ * DEC_BATCH, D), f32),
                   jax.ShapeDtypeStruct((DEC_SEQ * DEC_BATCH, D), f32)],
        scratch_shapes=[
            pltpu.VMEM((DEC_SEQ, DEC_BATCH, D), f32),
            pltpu.VMEM((DEC_SEQ, DEC_BATCH, D), f32),
            pltpu.VMEM((SAMPLE_ROWS, D), bf16),
            pltpu.VMEM((SAMPLE_ROWS, D), bf16),
            pltpu.VMEM((SAMPLE_ROWS, D), bf16),
            pltpu.VMEM((SAMPLE_ROWS, D), bf16),
            pltpu.VMEM((SAMPLE_ROWS, D), bf16),
            pltpu.VMEM((SAMPLE_ROWS, D), bf16),
            pltpu.VMEM((SAMPLE_ROWS, D), f32),
            pltpu.VMEM((SAMPLE_ROWS, D), f32),
            pltpu.VMEM((SAMPLE_ROWS, D), bf16),
        ],
        compiler_params=pltpu.CompilerParams(
            dimension_semantics=("arbitrary",), vmem_limit_bytes=VMEM_LIMIT),
        name="sample_layer",
    )(xs_t, hc, hp, mod, *weights)


def kernel(x_prompt, x_sample, state_conv, state_pool, c_prompt, c_sample, w_ada, b_ada, w_in,
           conv_w, conv_b, ln_g, ln_b, w_conv_out, pool_mix, pool_scale, w_pool_out, w_out,
           w_ff1, w_ff2, final_g):
    assert w_in.shape[0] == 1, "single layer"
    mod = _ada(c_sample, c_prompt, w_ada[0], b_ada[0])
    weights = (
        w_in[0].astype(bf16), w_conv_out[0].astype(bf16), pool_mix[0].astype(bf16),
        w_pool_out[0].astype(bf16), w_out[0].astype(bf16), w_ff1[0].astype(bf16), w_ff2[0].astype(bf16),
        conv_w[0], conv_b, ln_g, ln_b, pool_scale, final_g.reshape(1, D),
    )
    y_prompt, ncp, npp = _prompt_layer(x_prompt, mod, weights)
    new_conv_prompt = ncp[None, :, CONV_HIST - (CONV_W - 1):, :]
    new_pool_prompt = npp[None, :, POOL_HIST - (MAX_POOL - 1):, :]

    sc = state_conv[0]
    sp = state_pool[0]
    hc, hp = _sample_hist(sc.transpose(1, 0, 2), sp.transpose(1, 0, 2), conv_w[0])
    ys_t, a_new, u_new = _sample_layer(x_sample.transpose(1, 0, 2), hc, hp, mod, weights)
    a_new = a_new.reshape(DEC_SEQ, DEC_BATCH, D)
    u_new = u_new.reshape(DEC_SEQ, DEC_BATCH, D)
    y_sample = ys_t.transpose(1, 0, 2)
    new_conv_sample = jnp.concatenate([sc[:, DEC_SEQ:], a_new.transpose(1, 0, 2)], axis=1)[None]
    new_pool_sample = jnp.concatenate([sp[:, DEC_SEQ:], u_new.transpose(1, 0, 2)], axis=1)[None]
    return (y_prompt, y_sample, new_conv_prompt, new_pool_prompt, new_conv_sample, new_pool_sample)
```

```python
import jax
import jax.numpy as jnp
from jax import lax
from jax.experimental import pallas as pl
from jax.experimental.pallas import tpu as pltpu

D = 1024
BATCH = 8
SEQ = 2048
DEC_BATCH = 128
DEC_SEQ = 4
PAST_LEN = 16384
CONV_W = 31
POOL_WINDOWS = (2, 4, 8, 16)
POOL_GROUP = D // len(POOL_WINDOWS)
MAX_POOL = max(POOL_WINDOWS)
D_FF = 4 * D
N_MOD = 6
EPS = 1e-6

SUBLANES = 8
LANES = 128
N_LANE_GROUPS = D // LANES
ROW_CHUNK = 32
COL_CHUNK = 256
TT = 32
TQ = TT * BATCH
CONV_ROWS = TQ
CONV_HIST = 32
POOL_HIST = 16
VMEM_LIMIT = 58 * 1024 * 1024

f32 = jnp.float32
bf16 = jnp.bfloat16


def _dot(a, b):
    return jnp.dot(a, b, preferred_element_type=f32)


def _sigmoid(x):
    return 1.0 / (1.0 + jnp.exp(-x))


def _rms(x):
    return x * lax.rsqrt(jnp.mean(x * x, axis=-1, keepdims=True) + EPS)


def _const_spec(shape):
    nd = len(shape)
    return pl.BlockSpec(shape, lambda *_: (0,) * nd, pipeline_mode=pl.Buffered(1))


def _ada_kernel(cs_ref, cp_ref, w_ref, b_ref, o_ref):
    w = w_ref[...].astype(bf16)
    b = b_ref[0]
    cs = cs_ref[...]
    cp = cp_ref[...]
    o_ref[0, 0:DEC_BATCH, :] = _dot((cs * _sigmoid(cs)).astype(bf16), w) + b
    o_ref[0, DEC_BATCH:DEC_BATCH + BATCH, :] = _dot((cp * _sigmoid(cp)).astype(bf16), w) + b


def _ada(c_sample, c_prompt, w_ada, b_ada):
    return pl.pallas_call(
        _ada_kernel,
        grid=(N_MOD,),
        in_specs=[
            pl.BlockSpec((DEC_BATCH, D), lambda i: (0, 0)),
            pl.BlockSpec((BATCH, D), lambda i: (0, 0)),
            pl.BlockSpec((D, D), lambda i: (0, i)),
            pl.BlockSpec((1, 1, D), lambda i: (i, 0, 0)),
        ],
        out_specs=pl.BlockSpec((1, DEC_BATCH + BATCH, D), lambda i: (i, 0, 0)),
        out_shape=jax.ShapeDtypeStruct((N_MOD, DEC_BATCH + BATCH, D), f32),
        name="ada_mod",
    )(c_sample, c_prompt, w_ada, b_ada.reshape(N_MOD, 1, D))


def _mod_rows(mod_ref, i, r0, rb):
    if rb < ROW_CHUNK:
        m = mod_ref[i]
        return jnp.concatenate([m] * (ROW_CHUNK // rb), axis=0)
    return mod_ref[i, r0:r0 + ROW_CHUNK, :]


def _mod_cols(mod_ref, i, c0, rows, rb):
    m = mod_ref[i, :, c0:c0 + COL_CHUNK]
    if rb < rows:
        return jnp.concatenate([m] * (rows // rb), axis=0)
    return m


def _glu_and_pool_in(h_s, w_in, a_dst, u_dst):
    h = h_s[...]
    for c0 in range(0, D, COL_CHUNK):
        val = _dot(h, w_in[:, c0:c0 + COL_CHUNK])
        gate = _dot(h, w_in[:, D + c0:D + c0 + COL_CHUNK])
        a_dst[:, c0:c0 + COL_CHUNK] = val * _sigmoid(gate)
        u_dst[:, c0:c0 + COL_CHUNK] = _dot(h, w_in[:, 2 * D + c0:2 * D + c0 + COL_CHUNK])


def _ln_silu(cf, lg_ref, lb_ref):
    mu = jnp.mean(cf, axis=-1, keepdims=True)
    cc = cf - mu
    var = jnp.mean(cc * cc, axis=-1, keepdims=True)
    cn = cc * lax.rsqrt(var + EPS) * lg_ref[...] + lb_ref[...]
    return cn * _sigmoid(cn)


def _norm_mod(load_rows, dst, mod_ref, i_shift, i_scale, rows, rb):
    for r0 in range(0, rows, ROW_CHUNK):
        x = load_rows(r0)
        shift = _mod_rows(mod_ref, i_shift, r0, rb)
        scale = _mod_rows(mod_ref, i_scale, r0, rb)
        dst[r0:r0 + ROW_CHUNK, :] = (_rms(x) * (1.0 + scale) + shift).astype(bf16)


def _merge(rows, rb, load_x_cols, mod_ref, h_s, h2_s, cn_s, pl_s, pm_s, m_s, x1_s,
           w_in, w_co, pmix, w_po, w_out, ps_ref):
    for g in range(len(POOL_WINDOWS)):
        c0 = g * POOL_GROUP
        pm = _dot(pl_s[:, c0:c0 + POOL_GROUP], pmix[g]) * ps_ref[:, c0:c0 + POOL_GROUP]
        pm_s[:, c0:c0 + POOL_GROUP] = pm.astype(bf16)
    h = h_s[...]
    cn = cn_s[...]
    pm = pm_s[...]
    for c0 in range(0, D, COL_CHUNK):
        ga = _dot(h, w_in[:, 3 * D + c0:3 * D + c0 + COL_CHUNK])
        gb = _dot(h, w_in[:, 4 * D + c0:4 * D + c0 + COL_CHUNK])
        o_a = _dot(cn, w_co[:, c0:c0 + COL_CHUNK])
        o_b = _dot(pm, w_po[:, c0:c0 + COL_CHUNK])
        m_s[:, c0:c0 + COL_CHUNK] = (_sigmoid(ga) * o_a + _sigmoid(gb) * o_b).astype(bf16)
    m = m_s[...]
    for c0 in range(0, D, COL_CHUNK):
        g1 = _mod_cols(mod_ref, 2, c0, rows, rb)
        x1_s[:, c0:c0 + COL_CHUNK] = load_x_cols(c0) + g1 * _dot(m, w_out[:, c0:c0 + COL_CHUNK])
    _norm_mod(lambda r0: x1_s[r0:r0 + ROW_CHUNK, :], h2_s, mod_ref, 3, 4, rows, rb)


def _mlp(h2_s, hid_s, acc_s, w_f1, w_f2):
    h2 = h2_s[...]
    for f0 in range(0, D_FF, D):
        for c0 in range(0, D, COL_CHUNK):
            hid = jnp.maximum(_dot(h2, w_f1[:, f0 + c0:f0 + c0 + COL_CHUNK]), 0.0)
            hid_s[:, c0:c0 + COL_CHUNK] = (hid * hid).astype(bf16)
        hid = hid_s[...]
        for c0 in range(0, D, COL_CHUNK):
            part = _dot(hid, w_f2[f0:f0 + D, c0:c0 + COL_CHUNK])
            if f0 == 0:
                acc_s[:, c0:c0 + COL_CHUNK] = part
            else:
                acc_s[:, c0:c0 + COL_CHUNK] += part


def _final(rows, rb, mod_ref, x1_s, acc_s, fg_ref, store_y_rows):
    for r0 in range(0, rows, ROW_CHUNK):
        g2 = _mod_rows(mod_ref, 5, r0, rb)
        x2 = x1_s[r0:r0 + ROW_CHUNK, :] + g2 * acc_s[r0:r0 + ROW_CHUNK, :]
        store_y_rows(r0, _rms(x2) * fg_ref[...])


def _prompt_kernel(x_ref, mod_ref, w_in, w_co, pmix, w_po, w_out, w_f1, w_f2,
                   cw_ref, cb_ref, lg_ref, lb_ref, ps_ref, fg_ref,
                   y_ref, ncp_ref, npp_ref,
                   xrow, yrow, h_s, h2_s, a_ext, u_ext, cn_s, pl_s, pm_s, m_s, x1_s, acc_s, hid_s):
    j = pl.program_id(0)
    n_tiles = pl.num_programs(0) - 1
    a_h = CONV_HIST * BATCH
    u_h = POOL_HIST * BATCH

    @pl.when(j == 0)
    def _():
        a_ext[0:a_h, :] = jnp.zeros((a_h, D), f32)
        u_ext[0:u_h, :] = jnp.zeros((u_h, D), f32)
        h2_s[...] = jnp.zeros((TQ, D), bf16)
        x1_s[...] = jnp.zeros((TQ, D), f32)

    def store_y_rows(r0, y):
        for g in range(N_LANE_GROUPS):
            yrow[g, r0:r0 + ROW_CHUNK, :] = y[:, g * LANES:(g + 1) * LANES]

    def deinterleave(dst_ref, n_t):
        for b in range(BATCH):
            for t8 in range(0, n_t, SUBLANES):
                for g in range(N_LANE_GROUPS):
                    dst_ref[b, t8:t8 + SUBLANES, g * LANES:(g + 1) * LANES] = (
                        yrow[g, pl.ds(t8 * BATCH + b, SUBLANES, stride=BATCH), :])

    @pl.when(j < n_tiles)
    def _():
        for b in range(BATCH):
            for t8 in range(0, TT, SUBLANES):
                for g in range(N_LANE_GROUPS):
                    xrow[g, pl.ds(t8 * BATCH + b, SUBLANES, stride=BATCH), :] = (
                        x_ref[b, t8:t8 + SUBLANES, g * LANES:(g + 1) * LANES])

        def load_x_rows(r0):
            return jnp.concatenate([xrow[g, r0:r0 + ROW_CHUNK, :] for g in range(N_LANE_GROUPS)], axis=1)

        def load_x_cols(c0):
            g0 = c0 // LANES
            return jnp.concatenate([xrow[g0 + i] for i in range(COL_CHUNK // LANES)], axis=1)

        _norm_mod(load_x_rows, h_s, mod_ref, 0, 1, TQ, BATCH)
        _glu_and_pool_in(h_s, w_in, a_ext.at[a_h:a_h + TQ], u_ext.at[u_h:u_h + TQ])

        _mlp(h2_s, hid_s, acc_s, w_f1, w_f2)

        first = a_h - (CONV_W - 1) * BATCH
        rc = CONV_ROWS
        for r0 in range(0, TQ, rc):
            parts = []
            for c0 in range(0, D, COL_CHUNK):
                acc = jnp.zeros((rc, COL_CHUNK), f32)
                for k in range(CONV_W):
                    rk = r0 + first + k * BATCH
                    acc = acc + a_ext[rk:rk + rc, c0:c0 + COL_CHUNK] * cw_ref[k:k + 1, c0:c0 + COL_CHUNK]
                parts.append(acc)
            cf = jnp.concatenate(parts, axis=1) + cb_ref[...]
            cn_s[r0:r0 + rc, :] = _ln_silu(cf, lg_ref, lb_ref).astype(bf16)
            t_idx = j * TT + r0 // BATCH + lax.broadcasted_iota(jnp.int32, (rc, 1), 0) // BATCH
            parts = []
            for gi, w in enumerate(POOL_WINDOWS):
                c0 = gi * POOL_GROUP
                acc = jnp.zeros((rc, POOL_GROUP), f32)
                for jj in range(w):
                    rk = r0 + u_h - jj * BATCH
                    acc = acc + u_ext[rk:rk + rc, c0:c0 + POOL_GROUP]
                inv_cnt = 1.0 / jnp.minimum(t_idx + 1, w).astype(f32)
                parts.append(acc * inv_cnt - u_ext[r0 + u_h:r0 + u_h + rc, c0:c0 + POOL_GROUP])
            pl_s[r0:r0 + rc, :] = jnp.concatenate(parts, axis=1).astype(bf16)

        _final(TQ, BATCH, mod_ref, x1_s, acc_s, fg_ref, store_y_rows)
        deinterleave(y_ref, TT)

        _merge(TQ, BATCH, load_x_cols, mod_ref, h_s, h2_s, cn_s, pl_s, pm_s, m_s, x1_s,
               w_in, w_co, pmix, w_po, w_out, ps_ref)

        a_ext[0:a_h, :] = a_ext[TQ:TQ + a_h, :]
        u_ext[0:u_h, :] = u_ext[TQ:TQ + u_h, :]

    @pl.when(j == n_tiles)
    def _():
        _mlp(h2_s, hid_s, acc_s, w_f1, w_f2)
        _final(TQ, BATCH, mod_ref, x1_s, acc_s, fg_ref, store_y_rows)
        deinterleave(y_ref, TT)

    @pl.when(j == n_tiles - 1)
    def _():
        for g in range(N_LANE_GROUPS):
            yrow[g, 0:a_h, :] = a_ext[0:a_h, g * LANES:(g + 1) * LANES]
        deinterleave(ncp_ref, CONV_HIST)
        for g in range(N_LANE_GROUPS):
            yrow[g, 0:u_h, :] = u_ext[0:u_h, g * LANES:(g + 1) * LANES]
        deinterleave(npp_ref, POOL_HIST)


def _weight_specs():
    return [
        _const_spec((D, 5 * D)),
        _const_spec((D, D)),
        _const_spec((len(POOL_WINDOWS), POOL_GROUP, POOL_GROUP)),
        _const_spec((D, D)),
        _const_spec((D, D)),
        _const_spec((D, D_FF)),
        _const_spec((D_FF, D)),
        _const_spec((CONV_W, D)),
        _const_spec((1, D)),
        _const_spec((1, D)),
        _const_spec((1, D)),
        _const_spec((1, D)),
        _const_spec((1, D)),
    ]


def _prompt_layer(x_prompt, mod, weights):
    n_tiles = SEQ // TT
    return pl.pallas_call(
        _prompt_kernel,
        grid=(n_tiles + 1,),
        in_specs=[
            pl.BlockSpec((BATCH, TT, D), lambda j: (0, jnp.minimum(j, n_tiles - 1), 0)),
            pl.BlockSpec((N_MOD, BATCH, D), lambda j: (0, DEC_BATCH // BATCH, 0)),
        ] + _weight_specs(),
        out_specs=[
            pl.BlockSpec((BATCH, TT, D), lambda j: (0, jnp.maximum(j - 1, 0), 0)),
            pl.BlockSpec((BATCH, CONV_HIST, D), lambda j: (0, 0, 0)),
            pl.BlockSpec((BATCH, POOL_HIST, D), lambda j: (0, 0, 0)),
        ],
        out_shape=[
            jax.ShapeDtypeStruct((BATCH, SEQ, D), f32),
            jax.ShapeDtypeStruct((BATCH, CONV_HIST, D), f32),
            jax.ShapeDtypeStruct((BATCH, POOL_HIST, D), f32),
        ],
        scratch_shapes=[
            pltpu.VMEM((N_LANE_GROUPS, TQ, LANES), f32),
            pltpu.VMEM((N_LANE_GROUPS, TQ, LANES), f32),
            pltpu.VMEM((TQ, D), bf16),
            pltpu.VMEM((TQ, D), bf16),
            pltpu.VMEM(((CONV_HIST + TT) * BATCH, D), f32),
            pltpu.VMEM(((POOL_HIST + TT) * BATCH, D), f32),
            pltpu.VMEM((TQ, D), bf16),
            pltpu.VMEM((TQ, D), bf16),
            pltpu.VMEM((TQ, D), bf16),
            pltpu.VMEM((TQ, D), bf16),
            pltpu.VMEM((TQ, D), f32),
            pltpu.VMEM((TQ, D), f32),
            pltpu.VMEM((TQ, D), bf16),
        ],
        compiler_params=pltpu.CompilerParams(
            dimension_semantics=("arbitrary",), vmem_limit_bytes=VMEM_LIMIT),
        name="prompt_layer",
    )(x_prompt, mod, *weights)


HIST_ROWS = 16


def _hist_kernel(sc_ref, sp_ref, cw_ref, hc_ref, hp_ref):
    for t in range(DEC_SEQ):
        for c0 in range(0, D, COL_CHUNK):
            acc = jnp.zeros((HIST_ROWS, COL_CHUNK), f32)
            for jh in range(t, CONV_W - 1):
                acc = acc + sc_ref[jh, :, c0:c0 + COL_CHUNK] * cw_ref[jh - t:jh - t + 1, c0:c0 + COL_CHUNK]
            hc_ref[t, :, c0:c0 + COL_CHUNK] = acc
        for gi, w in enumerate(POOL_WINDOWS):
            c0 = gi * POOL_GROUP
            acc = jnp.zeros((HIST_ROWS, POOL_GROUP), f32)
            for jj in range(t + 1, w):
                acc = acc + sp_ref[MAX_POOL - 1 + t - jj, :, c0:c0 + POOL_GROUP]
            hp_ref[t, :, c0:c0 + POOL_GROUP] = acc


def _sample_hist(sc_t, sp_t, conv_w):
    return pl.pallas_call(
        _hist_kernel,
        grid=(DEC_BATCH // HIST_ROWS,),
        in_specs=[
            pl.BlockSpec((CONV_W - 1, HIST_ROWS, D), lambda i: (0, i, 0)),
            pl.BlockSpec((MAX_POOL - 1, HIST_ROWS, D), lambda i: (0, i, 0)),
            pl.BlockSpec((CONV_W, D), lambda i: (0, 0)),
        ],
        out_specs=[
            pl.BlockSpec((DEC_SEQ, HIST_ROWS, D), lambda i: (0, i, 0)),
            pl.BlockSpec((DEC_SEQ, HIST_ROWS, D), lambda i: (0, i, 0)),
        ],
        out_shape=[
            jax.ShapeDtypeStruct((DEC_SEQ, DEC_BATCH, D), f32),
            jax.ShapeDtypeStruct((DEC_SEQ, DEC_BATCH, D), f32),
        ],
        name="sample_hist",
    )(sc_t, sp_t, conv_w)


def _sample_kernel(x_ref, hc_ref, hp_ref, mod_ref, w_in, w_co, pmix, w_po, w_out, w_f1, w_f2,
                   cw_ref, cb_ref, lg_ref, lb_ref, ps_ref, fg_ref,
                   y_ref, an_ref, un_ref,
                   a_all, u_all, h_s, h2_s, cn_s, pl_s, pm_s, m_s, x1_s, acc_s, hid_s):
    t = pl.program_id(0)
    rows = DEC_BATCH

    _norm_mod(lambda r0: x_ref[0, r0:r0 + ROW_CHUNK, :], h_s, mod_ref, 0, 1, rows, rows)
    _glu_and_pool_in(h_s, w_in, a_all.at[t], u_all.at[t])
    an_ref[0] = a_all[t]
    un_ref[0] = u_all[t]

    for r0 in range(0, rows, ROW_CHUNK):
        cf = hc_ref[0, r0:r0 + ROW_CHUNK, :] + cb_ref[...]
        for d in range(DEC_SEQ):
            prev = a_all[jnp.maximum(t - d, 0), r0:r0 + ROW_CHUNK, :]
            k = CONV_W - 1 - d
            cf = cf + jnp.where(t >= d, prev * cw_ref[k:k + 1, :], 0.0)
        cn_s[r0:r0 + ROW_CHUNK, :] = _ln_silu(cf, lg_ref, lb_ref).astype(bf16)
        parts = []
        for gi, w in enumerate(POOL_WINDOWS):
            c0 = gi * POOL_GROUP
            acc = hp_ref[0, r0:r0 + ROW_CHUNK, c0:c0 + POOL_GROUP]
            for d in range(min(w, DEC_SEQ)):
                prev = u_all[jnp.maximum(t - d, 0), r0:r0 + ROW_CHUNK, c0:c0 + POOL_GROUP]
                acc = acc + jnp.where(t >= d, prev, 0.0)
            cnt = jnp.minimum(jnp.full((ROW_CHUNK, 1), PAST_LEN + 1, jnp.int32) + t, w).astype(f32)
            parts.append(acc * (1.0 / cnt) - u_all[t, r0:r0 + ROW_CHUNK, c0:c0 + POOL_GROUP])
        pl_s[r0:r0 + ROW_CHUNK, :] = jnp.concatenate(parts, axis=1).astype(bf16)

    def load_x_cols(c0):
        return x_ref[0, :, c0:c0 + COL_CHUNK]

    def store_y_rows(r0, y):
        y_ref[0, r0:r0 + ROW_CHUNK, :] = y

    _merge(rows, rows, load_x_cols, mod_ref, h_s, h2_s, cn_s, pl_s, pm_s, m_s, x1_s,
           w_in, w_co, pmix, w_po, w_out, ps_ref)
    _mlp(h2_s, hid_s, acc_s, w_f1, w_f2)
    _final(rows, rows, mod_ref, x1_s, acc_s, fg_ref, store_y_rows)


def _sample_layer(xs_t, hc, hp, mod, weights):
    slab = lambda: pl.BlockSpec((1, DEC_BATCH, D), lambda t: (t, 0, 0))
    return pl.pallas_call(
        _sample_kernel,
        grid=(DEC_SEQ,),
        in_specs=[slab(), slab(), slab(),
                  pl.BlockSpec((N_MOD, DEC_BATCH, D), lambda t: (0, 0, 0))] + _weight_specs(),
        out_specs=[slab(), slab(), slab()],
        out_shape=[jax.ShapeDtypeStruct((DEC_SEQ, DEC_BATCH, D), f32)] * 3,
        scratch_shapes=[
            pltpu.VMEM((DEC_SEQ, DEC_BATCH, D), f32),
            pltpu.VMEM((DEC_SEQ, DEC_BATCH, D), f32),
            pltpu.VMEM((DEC_BATCH, D), bf16),
            pltpu.VMEM((DEC_BATCH, D), bf16),
            pltpu.VMEM((DEC_BATCH, D), bf16),
            pltpu.VMEM((DEC_BATCH, D), bf16),
            pltpu.VMEM((DEC_BATCH, D), bf16),
            pltpu.VMEM((DEC_BATCH, D), bf16),
            pltpu.VMEM((DEC_BATCH, D), f32),
            pltpu.VMEM((DEC_BATCH, D), f32),
            pltpu.VMEM((DEC_BATCH, D), bf16),
        ],
        compiler_params=pltpu.CompilerParams(
            dimension_semantics=("arbitrary",), vmem_limit_bytes=VMEM_LIMIT),
        name="sample_layer",
    )(xs_t, hc, hp, mod, *weights)


def kernel(x_prompt, x_sample, state_conv, state_pool, c_prompt, c_sample, w_ada, b_ada, w_in,
           conv_w, conv_b, ln_g, ln_b, w_conv_out, pool_mix, pool_scale, w_pool_out, w_out,
           w_ff1, w_ff2, final_g):
    assert w_in.shape[0] == 1, "single layer"
    mod = _ada(c_sample, c_prompt, w_ada[0], b_ada[0])
    weights = (
        w_in[0].astype(bf16), w_conv_out[0].astype(bf16), pool_mix[0].astype(bf16),
        w_pool_out[0].astype(bf16), w_out[0].astype(bf16), w_ff1[0].astype(bf16), w_ff2[0].astype(bf16),
        conv_w[0], conv_b, ln_g, ln_b, pool_scale, final_g.reshape(1, D),
    )
    y_prompt, ncp, npp = _prompt_layer(x_prompt, mod, weights)
    new_conv_prompt = ncp[None, :, CONV_HIST - (CONV_W - 1):, :]
    new_pool_prompt = npp[None, :, POOL_HIST - (MAX_POOL - 1):, :]

    sc = state_conv[0]
    sp = state_pool[0]
    hc, hp = _sample_hist(sc.transpose(1, 0, 2), sp.transpose(1, 0, 2), conv_w[0])
    ys_t, a_new, u_new = _sample_layer(x_sample.transpose(1, 0, 2), hc, hp, mod, weights)
    y_sample = ys_t.transpose(1, 0, 2)
    new_conv_sample = jnp.concatenate([sc[:, DEC_SEQ:], a_new.transpose(1, 0, 2)], axis=1)[None]
    new_pool_sample = jnp.concatenate([sp[:, DEC_SEQ:], u_new.transpose(1, 0, 2)], axis=1)[None]
    return (y_prompt, y_sample, new_conv_prompt, new_pool_prompt, new_conv_sample, new_pool_sample)
```

```python
import jax
import jax.numpy as jnp
from jax import lax
from jax.experimental import pallas as pl
from jax.experimental.pallas import tpu as pltpu

D = 1024
BATCH = 8
SEQ = 2048
DEC_BATCH = 128
DEC_SEQ = 4
PAST_LEN = 16384
CONV_W = 31
POOL_WINDOWS = (2, 4, 8, 16)
POOL_GROUP = D // len(POOL_WINDOWS)
MAX_POOL = max(POOL_WINDOWS)
D_FF = 4 * D
N_MOD = 6
EPS = 1e-6

SUBLANES = 8
LANES = 128
N_LANE_GROUPS = D // LANES
ROW_CHUNK = 32
COL_CHUNK = 256
TT = 32
TQ = TT * BATCH
CONV_ROWS = TQ
CONV_COLS = 512
CONV_HIST = 32
POOL_HIST = 16
VMEM_LIMIT = 58 * 1024 * 1024

f32 = jnp.float32
bf16 = jnp.bfloat16


def _dot(a, b):
    return jnp.dot(a, b, preferred_element_type=f32)


def _sigmoid(x):
    return 1.0 / (1.0 + jnp.exp(-x))


def _rms(x):
    return x * lax.rsqrt(jnp.mean(x * x, axis=-1, keepdims=True) + EPS)


def _const_spec(shape):
    nd = len(shape)
    return pl.BlockSpec(shape, lambda *_: (0,) * nd, pipeline_mode=pl.Buffered(1))


def _ada_kernel(cs_ref, cp_ref, w_ref, b_ref, o_ref):
    w = w_ref[...].astype(bf16)
    b = b_ref[0]
    cs = cs_ref[...]
    cp = cp_ref[...]
    o_ref[0, 0:DEC_BATCH, :] = _dot((cs * _sigmoid(cs)).astype(bf16), w) + b
    o_ref[0, DEC_BATCH:DEC_BATCH + BATCH, :] = _dot((cp * _sigmoid(cp)).astype(bf16), w) + b


def _ada(c_sample, c_prompt, w_ada, b_ada):
    return pl.pallas_call(
        _ada_kernel,
        grid=(N_MOD,),
        in_specs=[
            pl.BlockSpec((DEC_BATCH, D), lambda i: (0, 0)),
            pl.BlockSpec((BATCH, D), lambda i: (0, 0)),
            pl.BlockSpec((D, D), lambda i: (0, i)),
            pl.BlockSpec((1, 1, D), lambda i: (i, 0, 0)),
        ],
        out_specs=pl.BlockSpec((1, DEC_BATCH + BATCH, D), lambda i: (i, 0, 0)),
        out_shape=jax.ShapeDtypeStruct((N_MOD, DEC_BATCH + BATCH, D), f32),
        name="ada_mod",
    )(c_sample, c_prompt, w_ada, b_ada.reshape(N_MOD, 1, D))


def _mod_rows(mod_ref, i, r0, rb):
    if rb < ROW_CHUNK:
        m = mod_ref[i]
        return jnp.concatenate([m] * (ROW_CHUNK // rb), axis=0)
    return mod_ref[i, r0:r0 + ROW_CHUNK, :]


def _mod_cols(mod_ref, i, c0, rows, rb):
    m = mod_ref[i, :, c0:c0 + COL_CHUNK]
    if rb < rows:
        return jnp.concatenate([m] * (rows // rb), axis=0)
    return m


def _glu_and_pool_in(h_s, w_in, a_dst, u_dst):
    h = h_s[...]
    for c0 in range(0, D, COL_CHUNK):
        val = _dot(h, w_in[:, c0:c0 + COL_CHUNK])
        gate = _dot(h, w_in[:, D + c0:D + c0 + COL_CHUNK])
        a_dst[:, c0:c0 + COL_CHUNK] = val * _sigmoid(gate)
        u_dst[:, c0:c0 + COL_CHUNK] = _dot(h, w_in[:, 2 * D + c0:2 * D + c0 + COL_CHUNK])


def _ln_silu(cf, lg_ref, lb_ref):
    mu = jnp.mean(cf, axis=-1, keepdims=True)
    cc = cf - mu
    var = jnp.mean(cc * cc, axis=-1, keepdims=True)
    cn = cc * lax.rsqrt(var + EPS) * lg_ref[...] + lb_ref[...]
    return cn * _sigmoid(cn)


def _norm_mod(load_rows, dst, mod_ref, i_shift, i_scale, rows, rb):
    for r0 in range(0, rows, ROW_CHUNK):
        x = load_rows(r0)
        shift = _mod_rows(mod_ref, i_shift, r0, rb)
        scale = _mod_rows(mod_ref, i_scale, r0, rb)
        dst[r0:r0 + ROW_CHUNK, :] = (_rms(x) * (1.0 + scale) + shift).astype(bf16)


def _merge(rows, rb, load_x_cols, mod_ref, h_s, h2_s, cn_s, pl_s, pm_s, m_s, x1_s,
           w_in, w_co, pmix, w_po, w_out, ps_ref):
    for g in range(len(POOL_WINDOWS)):
        c0 = g * POOL_GROUP
        pm = _dot(pl_s[:, c0:c0 + POOL_GROUP], pmix[g]) * ps_ref[:, c0:c0 + POOL_GROUP]
        pm_s[:, c0:c0 + POOL_GROUP] = pm.astype(bf16)
    h = h_s[...]
    cn = cn_s[...]
    pm = pm_s[...]
    for c0 in range(0, D, COL_CHUNK):
        ga = _dot(h, w_in[:, 3 * D + c0:3 * D + c0 + COL_CHUNK])
        gb = _dot(h, w_in[:, 4 * D + c0:4 * D + c0 + COL_CHUNK])
        o_a = _dot(cn, w_co[:, c0:c0 + COL_CHUNK])
        o_b = _dot(pm, w_po[:, c0:c0 + COL_CHUNK])
        m_s[:, c0:c0 + COL_CHUNK] = (_sigmoid(ga) * o_a + _sigmoid(gb) * o_b).astype(bf16)
    m = m_s[...]
    for c0 in range(0, D, COL_CHUNK):
        g1 = _mod_cols(mod_ref, 2, c0, rows, rb)
        x1_s[:, c0:c0 + COL_CHUNK] = load_x_cols(c0) + g1 * _dot(m, w_out[:, c0:c0 + COL_CHUNK])
    _norm_mod(lambda r0: x1_s[r0:r0 + ROW_CHUNK, :], h2_s, mod_ref, 3, 4, rows, rb)


def _mlp(h2_s, hid_s, acc_s, w_f1, w_f2):
    h2 = h2_s[...]
    for f0 in range(0, D_FF, D):
        for c0 in range(0, D, COL_CHUNK):
            hid = jnp.maximum(_dot(h2, w_f1[:, f0 + c0:f0 + c0 + COL_CHUNK]), 0.0)
            hid_s[:, c0:c0 + COL_CHUNK] = (hid * hid).astype(bf16)
        hid = hid_s[...]
        for c0 in range(0, D, COL_CHUNK):
            part = _dot(hid, w_f2[f0:f0 + D, c0:c0 + COL_CHUNK])
            if f0 == 0:
                acc_s[:, c0:c0 + COL_CHUNK] = part
            else:
                acc_s[:, c0:c0 + COL_CHUNK] += part


def _final(rows, rb, mod_ref, x1_s, acc_s, fg_ref, store_y_rows):
    for r0 in range(0, rows, ROW_CHUNK):
        g2 = _mod_rows(mod_ref, 5, r0, rb)
        x2 = x1_s[r0:r0 + ROW_CHUNK, :] + g2 * acc_s[r0:r0 + ROW_CHUNK, :]
        store_y_rows(r0, _rms(x2) * fg_ref[...])


def _prompt_kernel(x_ref, mod_ref, w_in, w_co, pmix, w_po, w_out, w_f1, w_f2,
                   cw_ref, cb_ref, lg_ref, lb_ref, ps_ref, fg_ref,
                   y_ref, ncp_ref, npp_ref,
                   xrow, yrow, h_s, h2_s, a_ext, u_ext, cn_s, pl_s, pm_s, m_s, x1_s, acc_s, hid_s):
    j = pl.program_id(0)
    n_tiles = pl.num_programs(0) - 1
    a_h = CONV_HIST * BATCH
    u_h = POOL_HIST * BATCH

    @pl.when(j == 0)
    def _():
        a_ext[0:a_h, :] = jnp.zeros((a_h, D), f32)
        u_ext[0:u_h, :] = jnp.zeros((u_h, D), f32)
        h2_s[...] = jnp.zeros((TQ, D), bf16)
        x1_s[...] = jnp.zeros((TQ, D), f32)

    def store_y_rows(r0, y):
        for g in range(N_LANE_GROUPS):
            yrow[g, r0:r0 + ROW_CHUNK, :] = y[:, g * LANES:(g + 1) * LANES]

    def deinterleave(dst_ref, n_t):
        for b in range(BATCH):
            for t8 in range(0, n_t, SUBLANES):
                for g in range(N_LANE_GROUPS):
                    dst_ref[b, t8:t8 + SUBLANES, g * LANES:(g + 1) * LANES] = (
                        yrow[g, pl.ds(t8 * BATCH + b, SUBLANES, stride=BATCH), :])

    @pl.when(j < n_tiles)
    def _():
        for b in range(BATCH):
            for t8 in range(0, TT, SUBLANES):
                for g in range(N_LANE_GROUPS):
                    xrow[g, pl.ds(t8 * BATCH + b, SUBLANES, stride=BATCH), :] = (
                        x_ref[b, t8:t8 + SUBLANES, g * LANES:(g + 1) * LANES])

        def load_x_rows(r0):
            return jnp.concatenate([xrow[g, r0:r0 + ROW_CHUNK, :] for g in range(N_LANE_GROUPS)], axis=1)

        def load_x_cols(c0):
            g0 = c0 // LANES
            return jnp.concatenate([xrow[g0 + i] for i in range(COL_CHUNK // LANES)], axis=1)

        _norm_mod(load_x_rows, h_s, mod_ref, 0, 1, TQ, BATCH)
        _glu_and_pool_in(h_s, w_in, a_ext.at[a_h:a_h + TQ], u_ext.at[u_h:u_h + TQ])

        _mlp(h2_s, hid_s, acc_s, w_f1, w_f2)

        first = a_h - (CONV_W - 1) * BATCH
        rc = CONV_ROWS
        for r0 in range(0, TQ, rc):
            parts = []
            for c0 in range(0, D, CONV_COLS):
                acc = jnp.zeros((rc, CONV_COLS), f32)
                for k in range(CONV_W):
                    rk = r0 + first + k * BATCH
                    acc = acc + a_ext[rk:rk + rc, c0:c0 + CONV_COLS] * cw_ref[k:k + 1, c0:c0 + CONV_COLS]
                parts.append(acc)
            cf = jnp.concatenate(parts, axis=1) + cb_ref[...]
            cn_s[r0:r0 + rc, :] = _ln_silu(cf, lg_ref, lb_ref).astype(bf16)
            t_idx = j * TT + r0 // BATCH + lax.broadcasted_iota(jnp.int32, (rc, 1), 0) // BATCH
            parts = []
            for gi, w in enumerate(POOL_WINDOWS):
                c0 = gi * POOL_GROUP
                acc = jnp.zeros((rc, POOL_GROUP), f32)
                for jj in range(w):
                    rk = r0 + u_h - jj * BATCH
                    acc = acc + u_ext[rk:rk + rc, c0:c0 + POOL_GROUP]
                inv_cnt = 1.0 / jnp.minimum(t_idx + 1, w).astype(f32)
                parts.append(acc * inv_cnt - u_ext[r0 + u_h:r0 + u_h + rc, c0:c0 + POOL_GROUP])
            pl_s[r0:r0 + rc, :] = jnp.concatenate(parts, axis=1).astype(bf16)

        _final(TQ, BATCH, mod_ref, x1_s, acc_s, fg_ref, store_y_rows)
        deinterleave(y_ref, TT)

        _merge(TQ, BATCH, load_x_cols, mod_ref, h_s, h2_s, cn_s, pl_s, pm_s, m_s, x1_s,
               w_in, w_co, pmix, w_po, w_out, ps_ref)

        a_ext[0:a_h, :] = a_ext[TQ:TQ + a_h, :]
        u_ext[0:u_h, :] = u_ext[TQ:TQ + u_h, :]

    @pl.when(j == n_tiles)
    def _():
        _mlp(h2_s, hid_s, acc_s, w_f1, w_f2)
        _final(TQ, BATCH, mod_ref, x1_s, acc_s, fg_ref, store_y_rows)
        deinterleave(y_ref, TT)

    @pl.when(j == n_tiles - 1)
    def _():
        for g in range(N_LANE_GROUPS):
            yrow[g, 0:a_h, :] = a_ext[0:a_h, g * LANES:(g + 1) * LANES]
        deinterleave(ncp_ref, CONV_HIST)
        for g in range(N_LANE_GROUPS):
            yrow[g, 0:u_h, :] = u_ext[0:u_h, g * LANES:(g + 1) * LANES]
        deinterleave(npp_ref, POOL_HIST)


def _weight_specs():
    return [
        _const_spec((D, 5 * D)),
        _const_spec((D, D)),
        _const_spec((len(POOL_WINDOWS), POOL_GROUP, POOL_GROUP)),
        _const_spec((D, D)),
        _const_spec((D, D)),
        _const_spec((D, D_FF)),
        _const_spec((D_FF, D)),
        _const_spec((CONV_W, D)),
        _const_spec((1, D)),
        _const_spec((1, D)),
        _const_spec((1, D)),
        _const_spec((1, D)),
        _const_spec((1, D)),
    ]


def _prompt_layer(x_prompt, mod, weights):
    n_tiles = SEQ // TT
    return pl.pallas_call(
        _prompt_kernel,
        grid=(n_tiles + 1,),
        in_specs=[
            pl.BlockSpec((BATCH, TT, D), lambda j: (0, jnp.minimum(j, n_tiles - 1), 0)),
            pl.BlockSpec((N_MOD, BATCH, D), lambda j: (0, DEC_BATCH // BATCH, 0)),
        ] + _weight_specs(),
        out_specs=[
            pl.BlockSpec((BATCH, TT, D), lambda j: (0, jnp.maximum(j - 1, 0), 0)),
            pl.BlockSpec((BATCH, CONV_HIST, D), lambda j: (0, 0, 0)),
            pl.BlockSpec((BATCH, POOL_HIST, D), lambda j: (0, 0, 0)),
        ],
        out_shape=[
            jax.ShapeDtypeStruct((BATCH, SEQ, D), f32),
            jax.ShapeDtypeStruct((BATCH, CONV_HIST, D), f32),
            jax.ShapeDtypeStruct((BATCH, POOL_HIST, D), f32),
        ],
        scratch_shapes=[
            pltpu.VMEM((N_LANE_GROUPS, TQ, LANES), f32),
            pltpu.VMEM((N_LANE_GROUPS, TQ, LANES), f32),
            pltpu.VMEM((TQ, D), bf16),
            pltpu.VMEM((TQ, D), bf16),
            pltpu.VMEM(((CONV_HIST + TT) * BATCH, D), f32),
            pltpu.VMEM(((POOL_HIST + TT) * BATCH, D), f32),
            pltpu.VMEM((TQ, D), bf16),
            pltpu.VMEM((TQ, D), bf16),
            pltpu.VMEM((TQ, D), bf16),
            pltpu.VMEM((TQ, D), bf16),
            pltpu.VMEM((TQ, D), f32),
            pltpu.VMEM((TQ, D), f32),
            pltpu.VMEM((TQ, D), bf16),
        ],
        compiler_params=pltpu.CompilerParams(
            dimension_semantics=("arbitrary",), vmem_limit_bytes=VMEM_LIMIT),
        name="prompt_layer",
    )(x_prompt, mod, *weights)


HIST_ROWS = 16


def _hist_kernel(sc_ref, sp_ref, cw_ref, hc_ref, hp_ref):
    for t in range(DEC_SEQ):
        for c0 in range(0, D, COL_CHUNK):
            acc = jnp.zeros((HIST_ROWS, COL_CHUNK), f32)
            for jh in range(t, CONV_W - 1):
                acc = acc + sc_ref[jh, :, c0:c0 + COL_CHUNK] * cw_ref[jh - t:jh - t + 1, c0:c0 + COL_CHUNK]
            hc_ref[t, :, c0:c0 + COL_CHUNK] = acc
        for gi, w in enumerate(POOL_WINDOWS):
            c0 = gi * POOL_GROUP
            acc = jnp.zeros((HIST_ROWS, POOL_GROUP), f32)
            for jj in range(t + 1, w):
                acc = acc + sp_ref[MAX_POOL - 1 + t - jj, :, c0:c0 + POOL_GROUP]
            hp_ref[t, :, c0:c0 + POOL_GROUP] = acc


def _sample_hist(sc_t, sp_t, conv_w):
    return pl.pallas_call(
        _hist_kernel,
        grid=(DEC_BATCH // HIST_ROWS,),
        in_specs=[
            pl.BlockSpec((CONV_W - 1, HIST_ROWS, D), lambda i: (0, i, 0)),
            pl.BlockSpec((MAX_POOL - 1, HIST_ROWS, D), lambda i: (0, i, 0)),
            pl.BlockSpec((CONV_W, D), lambda i: (0, 0)),
        ],
        out_specs=[
            pl.BlockSpec((DEC_SEQ, HIST_ROWS, D), lambda i: (0, i, 0)),
            pl.BlockSpec((DEC_SEQ, HIST_ROWS, D), lambda i: (0, i, 0)),
        ],
        out_shape=[
            jax.ShapeDtypeStruct((DEC_SEQ, DEC_BATCH, D), f32),
            jax.ShapeDtypeStruct((DEC_SEQ, DEC_BATCH, D), f32),
        ],
        name="sample_hist",
    )(sc_t, sp_t, conv_w)


def _sample_kernel(x_ref, hc_ref, hp_ref, mod_ref, w_in, w_co, pmix, w_po, w_out, w_f1, w_f2,
                   cw_ref, cb_ref, lg_ref, lb_ref, ps_ref, fg_ref,
                   y_ref, an_ref, un_ref,
                   a_all, u_all, h_s, h2_s, cn_s, pl_s, pm_s, m_s, x1_s, acc_s, hid_s):
    t = pl.program_id(0)
    rows = DEC_BATCH

    _norm_mod(lambda r0: x_ref[0, r0:r0 + ROW_CHUNK, :], h_s, mod_ref, 0, 1, rows, rows)
    _glu_and_pool_in(h_s, w_in, a_all.at[t], u_all.at[t])
    an_ref[0] = a_all[t]
    un_ref[0] = u_all[t]

    for r0 in range(0, rows, ROW_CHUNK):
        cf = hc_ref[0, r0:r0 + ROW_CHUNK, :] + cb_ref[...]
        for d in range(DEC_SEQ):
            prev = a_all[jnp.maximum(t - d, 0), r0:r0 + ROW_CHUNK, :]
            k = CONV_W - 1 - d
            cf = cf + jnp.where(t >= d, prev * cw_ref[k:k + 1, :], 0.0)
        cn_s[r0:r0 + ROW_CHUNK, :] = _ln_silu(cf, lg_ref, lb_ref).astype(bf16)
        parts = []
        for gi, w in enumerate(POOL_WINDOWS):
            c0 = gi * POOL_GROUP
            acc = hp_ref[0, r0:r0 + ROW_CHUNK, c0:c0 + POOL_GROUP]
            for d in range(min(w, DEC_SEQ)):
                prev = u_all[jnp.maximum(t - d, 0), r0:r0 + ROW_CHUNK, c0:c0 + POOL_GROUP]
                acc = acc + jnp.where(t >= d, prev, 0.0)
            cnt = jnp.minimum(jnp.full((ROW_CHUNK, 1), PAST_LEN + 1, jnp.int32) + t, w).astype(f32)
            parts.append(acc * (1.0 / cnt) - u_all[t, r0:r0 + ROW_CHUNK, c0:c0 + POOL_GROUP])
        pl_s[r0:r0 + ROW_CHUNK, :] = jnp.concatenate(parts, axis=1).astype(bf16)

    def load_x_cols(c0):
        return x_ref[0, :, c0:c0 + COL_CHUNK]

    def store_y_rows(r0, y):
        y_ref[0, r0:r0 + ROW_CHUNK, :] = y

    _merge(rows, rows, load_x_cols, mod_ref, h_s, h2_s, cn_s, pl_s, pm_s, m_s, x1_s,
           w_in, w_co, pmix, w_po, w_out, ps_ref)
    _mlp(h2_s, hid_s, acc_s, w_f1, w_f2)
    _final(rows, rows, mod_ref, x1_s, acc_s, fg_ref, store_y_rows)


def _sample_layer(xs_t, hc, hp, mod, weights):
    slab = lambda: pl.BlockSpec((1, DEC_BATCH, D), lambda t: (t, 0, 0))
    return pl.pallas_call(
        _sample_kernel,
        grid=(DEC_SEQ,),
        in_specs=[slab(), slab(), slab(),
                  pl.BlockSpec((N_MOD, DEC_BATCH, D), lambda t: (0, 0, 0))] + _weight_specs(),
        out_specs=[slab(), slab(), slab()],
        out_shape=[jax.ShapeDtypeStruct((DEC_SEQ, DEC_BATCH, D), f32)] * 3,
        scratch_shapes=[
            pltpu.VMEM((DEC_SEQ, DEC_BATCH, D), f32),
            pltpu.VMEM((DEC_SEQ, DEC_BATCH, D), f32),
            pltpu.VMEM((DEC_BATCH, D), bf16),
            pltpu.VMEM((DEC_BATCH, D), bf16),
            pltpu.VMEM((DEC_BATCH, D), bf16),
            pltpu.VMEM((DEC_BATCH, D), bf16),
            pltpu.VMEM((DEC_BATCH, D), bf16),
            pltpu.VMEM((DEC_BATCH, D), bf16),
            pltpu.VMEM((DEC_BATCH, D), f32),
            pltpu.VMEM((DEC_BATCH, D), f32),
            pltpu.VMEM((DEC_BATCH, D), bf16),
        ],
        compiler_params=pltpu.CompilerParams(
            dimension_semantics=("arbitrary",), vmem_limit_bytes=VMEM_LIMIT),
        name="sample_layer",
    )(xs_t, hc, hp, mod, *weights)


def kernel(x_prompt, x_sample, state_conv, state_pool, c_prompt, c_sample, w_ada, b_ada, w_in,
           conv_w, conv_b, ln_g, ln_b, w_conv_out, pool_mix, pool_scale, w_pool_out, w_out,
           w_ff1, w_ff2, final_g):
    assert w_in.shape[0] == 1, "single layer"
    mod = _ada(c_sample, c_prompt, w_ada[0], b_ada[0])
    weights = (
        w_in[0].astype(bf16), w_conv_out[0].astype(bf16), pool_mix[0].astype(bf16),
        w_pool_out[0].astype(bf16), w_out[0].astype(bf16), w_ff1[0].astype(bf16), w_ff2[0].astype(bf16),
        conv_w[0], conv_b, ln_g, ln_b, pool_scale, final_g.reshape(1, D),
    )
    y_prompt, ncp, npp = _prompt_layer(x_prompt, mod, weights)
    new_conv_prompt = ncp[None, :, CONV_HIST - (CONV_W - 1):, :]
    new_pool_prompt = npp[None, :, POOL_HIST - (MAX_POOL - 1):, :]

    sc = state_conv[0]
    sp = state_pool[0]
    hc, hp = _sample_hist(sc.transpose(1, 0, 2), sp.transpose(1, 0, 2), conv_w[0])
    ys_t, a_new, u_new = _sample_layer(x_sample.transpose(1, 0, 2), hc, hp, mod, weights)
    y_sample = ys_t.transpose(1, 0, 2)
    new_conv_sample = jnp.concatenate([sc[:, DEC_SEQ:], a_new.transpose(1, 0, 2)], axis=1)[None]
    new_pool_sample = jnp.concatenate([sp[:, DEC_SEQ:], u_new.transpose(1, 0, 2)], axis=1)[None]
    return (y_prompt, y_sample, new_conv_prompt, new_pool_prompt, new_conv_sample, new_pool_sample)
```
